```python
import math
import jax, jax.numpy as jnp
from jax import lax
import numpy as np

D_MODEL = 1024
BATCH = 8
SEQ = 8192
DEPTH = 4

CHUNK = 64
Q_BLOCK = 128
ATT_HEADS = 8
ATT_HEAD_DIM = 64
IDX_HEADS = 8
IDX_DIM = 64
TOPK_MAX = 256
ROPE_THETA = 500000.0
ROPE_FRACTION_DIV = 4
GLA_HEADS = 4
GLA_DK = 128
GLA_DV = 256
GLA_GATE_RANK = 16
GLA_TAU = 16.0
D_FF = 2816
CONV_WIDTH = 3
LN_EPS = 1e-5
DN_ALPHA = (2.0 * DEPTH) ** 0.25
DN_BETA = (8.0 * DEPTH) ** -0.25

ATT_WIDTH = ATT_HEADS * ATT_HEAD_DIM
IDXQ_WIDTH = IDX_HEADS * IDX_DIM
GLA_K_WIDTH = GLA_HEADS * GLA_DK
GLA_V_WIDTH = GLA_HEADS * GLA_DV
IN_SPLITS = (ATT_WIDTH, ATT_WIDTH, ATT_WIDTH,
             IDXQ_WIDTH, IDX_DIM, IDX_HEADS,
             GLA_K_WIDTH, GLA_K_WIDTH, GLA_V_WIDTH, GLA_GATE_RANK, GLA_V_WIDTH,
             D_MODEL, D_MODEL)
IN_WIDTH = sum(IN_SPLITS)

kernel_name = "hybrid_dsa_gla_convffn_deepnorm"

F32 = jnp.float32


def layer_norm(x, g, b):
    xf = x.astype(F32)
    mu = jnp.mean(xf, axis=-1, keepdims=True)
    var = jnp.mean(jnp.square(xf - mu), axis=-1, keepdims=True)
    return ((xf - mu) * lax.rsqrt(var + LN_EPS) * g.astype(F32) + b.astype(F32)).astype(x.dtype)


def partial_rope(x, pos):
    dh = x.shape[-1]
    rd = dh // ROPE_FRACTION_DIV
    half = rd // 2
    inv = ROPE_THETA ** (-jnp.arange(half, dtype=F32) * 2.0 / rd)
    ang = pos.astype(F32)[..., None] * inv
    cos = jnp.cos(ang)[:, :, None, :]
    sin = jnp.sin(ang)[:, :, None, :]
    xr = x[..., :rd].astype(F32)
    x1, x2 = xr[..., :half], xr[..., half:]
    rot = jnp.concatenate([x1 * cos - x2 * sin, x2 * cos + x1 * sin], axis=-1)
    return jnp.concatenate([rot.astype(x.dtype), x[..., rd:]], axis=-1)


def dsa_attention(q, k, v, q_idx, k_idx, w_idx):
    B, S, H, Dh = q.shape
    topk = min(TOPK_MAX, S // 4)
    nb = S // Q_BLOCK
    key_chunk = jnp.arange(S) // CHUNK
    k_idx32 = k_idx.astype(F32)

    def to_blocks(a):
        return jnp.moveaxis(a.reshape((B, nb, Q_BLOCK) + a.shape[2:]), 1, 0)

    def block(args):
        qb, qib, wb, start = args
        q_chunk = (start + jnp.arange(Q_BLOCK)) // CHUNK
        admissible = key_chunk[None, :] <= q_chunk[:, None]
        dots = jnp.einsum('bqhd,bsd->bqhs', qib.astype(F32), k_idx32) * (IDX_DIM ** -0.5)
        score = jnp.einsum('bqhs,bqh->bqs', jax.nn.relu(dots),
                           wb.astype(F32) * (IDX_HEADS ** -0.5))
        score = jnp.where(admissible[None], score, -jnp.inf)
        top_val, top_idx = lax.top_k(score, topk)
        valid = jnp.isfinite(top_val)
        k_sel = jax.vmap(lambda kk, ii: kk[ii])(k, top_idx)
        v_sel = jax.vmap(lambda vv, ii: vv[ii])(v, top_idx)
        logits = jnp.einsum('bqhd,bqkhd->bhqk', qb.astype(F32), k_sel.astype(F32)) * (Dh ** -0.5)
        logits = jnp.where(valid[:, None], logits, -jnp.inf)
        p = jax.nn.softmax(logits, axis=-1)
        return jnp.einsum('bhqk,bqkhd->bqhd', p, v_sel.astype(F32)).astype(q.dtype)

    starts = jnp.arange(nb) * Q_BLOCK
    out = lax.map(block, (to_blocks(q), to_blocks(q_idx), to_blocks(w_idx), starts))
    return jnp.moveaxis(out, 0, 1).reshape(B, S, H, Dh)


def gla_chunked(q, k, v, log_a):
    B, S, H, Dk = q.shape
    Dv = v.shape[-1]
    nc = S // CHUNK

    def chunks(a):
        return jnp.moveaxis(a.astype(F32).reshape((B, nc, CHUNK) + a.shape[2:]), 1, 0)

    def step(state, inp):
        qc, kc, vc, lac = inp
        cum = jnp.cumsum(lac, axis=1)
        total = cum[:, -1]
        k_dec = kc * jnp.exp(total[:, None] - cum)
        state = jnp.exp(total)[..., None] * state + jnp.einsum('bchk,bchv->bhkv', k_dec, vc)
        out = jnp.einsum('bchk,bhkv->bchv', qc, state)
        return state, out

    s0 = jnp.zeros((B, H, Dk, Dv), F32)
    _, o = lax.scan(step, s0, (chunks(q) * (Dk ** -0.5), chunks(k), chunks(v), chunks(log_a)))
    return jnp.moveaxis(o, 0, 1).reshape(B, S, H, Dv)


def hybrid_mixer(x, pos, w_in, gla_w_gate, gla_b_gate, gla_norm_g, p_attn, p_gla, w_out, b_out):
    B, S, _ = x.shape
    proj = x @ w_in
    offsets = np.cumsum(IN_SPLITS)[:-1].tolist()
    (aq, ak, av, iq, ik, iw, gq, gk, gv, g_lr, g_r, gate_a, gate_b) = jnp.split(proj, offsets, axis=-1)

    aq = partial_rope(aq.reshape(B, S, ATT_HEADS, ATT_HEAD_DIM), pos)
    ak = partial_rope(ak.reshape(B, S, ATT_HEADS, ATT_HEAD_DIM), pos)
    av = av.reshape(B, S, ATT_HEADS, ATT_HEAD_DIM)
    iq = partial_rope(iq.reshape(B, S, IDX_HEADS, IDX_DIM), pos)
    ik = partial_rope(ik[:, :, None, :], pos)[:, :, 0, :]
    ya = dsa_attention(aq, ak, av, iq, ik, iw).reshape(B, S, ATT_WIDTH) @ p_attn

    gate_logits = g_lr @ gla_w_gate + gla_b_gate
    log_a = jax.nn.log_sigmoid(gate_logits.astype(F32)) / GLA_TAU
    o = gla_chunked(gq.reshape(B, S, GLA_HEADS, GLA_DK),
                    gk.reshape(B, S, GLA_HEADS, GLA_DK),
                    gv.reshape(B, S, GLA_HEADS, GLA_DV),
                    log_a.reshape(B, S, GLA_HEADS, GLA_DK))
    mu = jnp.mean(o, axis=-1, keepdims=True)
    var = jnp.mean(jnp.square(o - mu), axis=-1, keepdims=True)
    o = (o - mu) * lax.rsqrt(var + LN_EPS) * gla_norm_g.astype(F32).reshape(GLA_HEADS, GLA_DV)
    o = o.reshape(B, S, GLA_V_WIDTH).astype(x.dtype) * jax.nn.silu(g_r)
    yb = o @ p_gla

    merged = jax.nn.sigmoid(gate_a) * ya + jax.nn.sigmoid(gate_b) * yb
    return merged @ w_out + b_out


def conv_ffn(x, w_up, conv_w, conv_b, w_down):
    u = x @ w_up
    c = u.shape[-1]
    u = lax.conv_general_dilated(u, conv_w[:, None, :], window_strides=(1,),
                                 padding=[(CONV_WIDTH - 1, 0)],
                                 dimension_numbers=('NWC', 'WIO', 'NWC'),
                                 feature_group_count=c) + conv_b
    a, b = jnp.split(u, 2, axis=-1)
    return (jax.nn.gelu(a) * b) @ w_down


def setup_inputs(seed: int = 0) -> dict:
    key = jax.random.key(seed)
    ks = jax.random.split(key, 20)
    L, D = DEPTH, D_MODEL
    nrm = lambda k, shape, s: jax.random.normal(k, shape, F32) * s
    x = jax.random.normal(ks[0], (BATCH, SEQ, D), F32)
    offs = jax.random.randint(ks[1], (BATCH, 1), 0, 64) * CHUNK
    positions = (offs + jnp.arange(SEQ, dtype=jnp.int32)[None, :]).astype(jnp.int32)
    return {
        "x": x,
        "positions": positions,
        "w_in": nrm(ks[2], (L, D, IN_WIDTH), D ** -0.5),
        "gla_w_gate": nrm(ks[3], (L, GLA_GATE_RANK, GLA_K_WIDTH), GLA_GATE_RANK ** -0.5),
        "gla_b_gate": nrm(ks[4], (L, GLA_K_WIDTH), 0.1) + 1.0,
        "gla_norm_g": 1.0 + nrm(ks[5], (L, GLA_V_WIDTH), 0.02),
        "p_attn": nrm(ks[6], (L, ATT_WIDTH, D), ATT_WIDTH ** -0.5),
        "p_gla": nrm(ks[7], (L, GLA_V_WIDTH, D), GLA_V_WIDTH ** -0.5),
        "w_mix_out": nrm(ks[8], (L, D, D), DN_BETA * D ** -0.5),
        "b_mix_out": nrm(ks[9], (L, D), 0.02),
        "ln1_g": 1.0 + nrm(ks[10], (L, D), 0.02),
        "ln1_b": nrm(ks[11], (L, D), 0.02),
        "w_up": nrm(ks[12], (L, D, 2 * D_FF), D ** -0.5),
        "conv_w": nrm(ks[13], (L, CONV_WIDTH, 2 * D_FF), CONV_WIDTH ** -0.5),
        "conv_b": nrm(ks[14], (L, 2 * D_FF), 0.02),
        "w_down": nrm(ks[15], (L, D_FF, D), DN_BETA * D_FF ** -0.5),
        "ln2_g": 1.0 + nrm(ks[16], (L, D), 0.02),
        "ln2_b": nrm(ks[17], (L, D), 0.02),
    }


def reference(x, positions, w_in, gla_w_gate, gla_b_gate, gla_norm_g, p_attn, p_gla,
              w_mix_out, b_mix_out, ln1_g, ln1_b, w_up, conv_w, conv_b, w_down, ln2_g, ln2_b):
    for l in range(DEPTH):
        y = hybrid_mixer(x, positions, w_in[l], gla_w_gate[l], gla_b_gate[l], gla_norm_g[l],
                         p_attn[l], p_gla[l], w_mix_out[l], b_mix_out[l])
        x = layer_norm(DN_ALPHA * x + y, ln1_g[l], ln1_b[l])
        y = conv_ffn(x, w_up[l], conv_w[l], conv_b[l], w_down[l])
        x = layer_norm(DN_ALPHA * x + y, ln2_g[l], ln2_b[l])
    return x
```

```python
import functools

import jax
import jax.numpy as jnp
import numpy as np
from jax import lax
from jax.experimental import pallas as pl
from jax.experimental.pallas import tpu as pltpu

F32 = jnp.float32
BF16 = jnp.bfloat16

DEPTH = 4
CHUNK = 64
ATT_HEADS = 8
ATT_HEAD_DIM = 64
IDX_HEADS = 8
IDX_DIM = 64
TOPK_MAX = 256
ROPE_THETA = 500000.0
ROPE_DIMS = ATT_HEAD_DIM // 4
ROPE_HALF = ROPE_DIMS // 2
GLA_HEADS = 4
GLA_DK = 128
GLA_DV = 256
GLA_GATE_RANK = 16
GLA_TAU = 16.0
CONV_WIDTH = 3
LN_EPS = 1e-5
DN_ALPHA = (2.0 * DEPTH) ** 0.25

ATT_WIDTH = ATT_HEADS * ATT_HEAD_DIM
GLA_K_WIDTH = GLA_HEADS * GLA_DK
GLA_V_WIDTH = GLA_HEADS * GLA_DV

LANES = 128
SUBLANES = 8
VMEM_LIMIT_BYTES = 56 * 1024 * 1024

NEG_BIG = -1e30

PROJ_TOKENS = 512
DSA_QB = 256
DSA_TS = 256
GLA_TOKENS = 512
MIX_TOKENS = 512
FFN_TOKENS = 512
BISECT_MAX_ITERS = 64


def _cparams(sem):
    return pltpu.CompilerParams(dimension_semantics=sem, vmem_limit_bytes=VMEM_LIMIT_BYTES)


def _dot(a, b):
    return jnp.dot(a, b, preferred_element_type=F32)


def _dot_nt(a, b):
    return lax.dot_general(a, b, (((1,), (1,)), ((), ())), preferred_element_type=F32)


def _dot_tn(a, b):
    return lax.dot_general(a, b, (((0,), (0,)), ((), ())), preferred_element_type=F32)


def _layer_norm_rows(z, g, b):
    mu = jnp.mean(z, axis=-1, keepdims=True)
    zc = z - mu
    var = jnp.mean(zc * zc, axis=-1, keepdims=True)
    return zc * lax.rsqrt(var + LN_EPS) * g + b


def _rope_feature_major(y, cos, sin, heads):
    t = y.shape[-1]
    y3 = y.reshape(heads, ATT_HEAD_DIM, t)
    x1 = y3[:, 0:ROPE_HALF, :]
    x2 = y3[:, ROPE_HALF:ROPE_DIMS, :]
    r1 = x1 * cos[None] - x2 * sin[None]
    r2 = x2 * cos[None] + x1 * sin[None]
    out = jnp.concatenate([r1, r2, y3[:, ROPE_DIMS:, :]], axis=1)
    return out.reshape(heads * ATT_HEAD_DIM, t)


def _attn_proj_kernel(x_ref, pos_ref, inv_ref, wq_ref, wk_ref, wv_ref, wiq_ref, wsm_ref,
                      aqT_ref, iqT_ref, vT_ref, k_ref, kidx_ref, wT_ref):
    xb = x_ref[0].astype(BF16)
    ang = pos_ref[0].astype(F32) * inv_ref[...]
    cos = jnp.cos(ang)
    sin = jnp.sin(ang)

    q = _rope_feature_major(_dot_nt(wq_ref[...], xb), cos, sin, ATT_HEADS)
    aqT_ref[0] = (q * (ATT_HEAD_DIM ** -0.5)).astype(BF16)
    iq = _rope_feature_major(_dot_nt(wiq_ref[...], xb), cos, sin, IDX_HEADS)
    iqT_ref[0] = (iq * (IDX_DIM ** -0.5)).astype(BF16)
    vT_ref[0] = _dot_nt(wv_ref[...], xb).astype(BF16)
    k = _rope_feature_major(_dot_nt(wk_ref[...], xb), cos, sin, ATT_HEADS)
    k_ref[0] = k.T.astype(BF16)

    sm = _dot_nt(wsm_ref[...], xb)
    ik = _rope_feature_major(sm[0:IDX_DIM], cos, sin, 1)
    row = lax.broadcasted_iota(jnp.int32, sm.shape, 0)
    ik_full = jnp.where(row < IDX_DIM, jnp.concatenate([ik, sm[IDX_DIM:]], axis=0), 0.0)
    kidx_ref[0] = ik_full.T.astype(BF16)
    wT_ref[0] = sm[IDX_DIM:IDX_DIM + IDX_HEADS] * (IDX_HEADS ** -0.5)


def _attn_proj(x, pos3, inv, wq, wk, wv, wiq, wsm):
    B, S, D = x.shape
    T = min(PROJ_TOKENS, S)
    grid = (B, S // T)
    full = lambda a: pl.BlockSpec(a.shape, lambda b, t: (0,) * a.ndim)
    fm = pl.BlockSpec((1, ATT_WIDTH, T), lambda b, t: (b, 0, t))
    out_shape = (
        jax.ShapeDtypeStruct((B, ATT_WIDTH, S), BF16),
        jax.ShapeDtypeStruct((B, ATT_WIDTH, S), BF16),
        jax.ShapeDtypeStruct((B, ATT_WIDTH, S), BF16),
        jax.ShapeDtypeStruct((B, S, ATT_WIDTH), BF16),
        jax.ShapeDtypeStruct((B, S, LANES), BF16),
        jax.ShapeDtypeStruct((B, IDX_HEADS, S), F32),
    )
    return pl.pallas_call(
        _attn_proj_kernel,
        grid=grid,
        in_specs=[
            pl.BlockSpec((1, T, D), lambda b, t: (b, t, 0)),
            pl.BlockSpec((1, 1, T), lambda b, t: (b, 0, t)),
            full(inv), full(wq), full(wk), full(wv), full(wiq), full(wsm),
        ],
        out_specs=(
            fm, fm, fm,
            pl.BlockSpec((1, T, ATT_WIDTH), lambda b, t: (b, t, 0)),
            pl.BlockSpec((1, T, LANES), lambda b, t: (b, t, 0)),
            pl.BlockSpec((1, IDX_HEADS, T), lambda b, t: (b, 0, t)),
        ),
        out_shape=out_shape,
        compiler_params=_cparams(("parallel", "parallel")),
        name="attn_proj",
    )(x, pos3, inv, wq, wk, wv, wiq, wsm)


GLA_PROJ_CHUNK = 1024


def _gla_proj_kernel(x_ref, w_ref, o_ref):
    xb = x_ref[0].astype(BF16)
    width = w_ref.shape[1]
    for c0 in range(0, width, GLA_PROJ_CHUNK):
        c1 = min(c0 + GLA_PROJ_CHUNK, width)
        o_ref[0, :, c0:c1] = _dot(xb, w_ref[:, c0:c1]).astype(BF16)


def _gla_proj(x, wg):
    B, S, D = x.shape
    T = min(PROJ_TOKENS, S)
    W = wg.shape[1]
    return pl.pallas_call(
        _gla_proj_kernel,
        grid=(B, S // T),
        in_specs=[pl.BlockSpec((1, T, D), lambda b, t: (b, t, 0)),
                  pl.BlockSpec(wg.shape, lambda b, t: (0, 0))],
        out_specs=pl.BlockSpec((1, T, W), lambda b, t: (b, t, 0)),
        out_shape=jax.ShapeDtypeStruct((B, S, W), BF16),
        compiler_params=_cparams(("parallel", "parallel")),
        name="gla_proj",
    )(x, wg)


def _dsa_kernel(topk, iqT_ref, aqT_ref, wT_ref, kidx_ref, k_ref, vT_ref, o_ref,
                sc_ref, bias_ref, iqz_ref, aqz_ref, m_ref, l_ref, acc_ref):
    QB = DSA_QB
    TS = DSA_TS
    qi = pl.program_id(1)
    n_tiles = qi + 1
    q0 = qi * QB

    zeros64 = jnp.zeros((IDX_DIM, QB), BF16)
    for h in range(IDX_HEADS):
        iqz_ref[h, 0:IDX_DIM, :] = iqT_ref[0, h * IDX_DIM:(h + 1) * IDX_DIM, :]
        iqz_ref[h, IDX_DIM:, :] = zeros64
    for h in range(ATT_HEADS):
        lo_half = (h % 2) * ATT_HEAD_DIM
        hi_half = (1 - h % 2) * ATT_HEAD_DIM
        aqz_ref[h, lo_half:lo_half + ATT_HEAD_DIM, :] = aqT_ref[0, h * ATT_HEAD_DIM:(h + 1) * ATT_HEAD_DIM, :]
        aqz_ref[h, hi_half:hi_half + ATT_HEAD_DIM, :] = zeros64

    q_chunk = (q0 + lax.broadcasted_iota(jnp.int32, (1, QB), 1)) // CHUNK

    def score_tile(j, diag):
        s0 = pl.multiple_of(j * TS, TS)
        kt = kidx_ref[0, pl.ds(s0, TS), :]
        acc = jnp.zeros((TS, QB), F32)
        for h in range(IDX_HEADS):
            d = _dot(kt, iqz_ref[h])
            acc = acc + wT_ref[0, h:h + 1, :] * jnp.maximum(d, 0.0)
        if diag:
            s_chunk = (s0 + lax.broadcasted_iota(jnp.int32, (TS, 1), 0)) // CHUNK
            adm = s_chunk <= q_chunk
            lo_t = jnp.min(jnp.where(adm, acc, jnp.inf), axis=0, keepdims=True)
            hi_t = jnp.max(jnp.where(adm, acc, -jnp.inf), axis=0, keepdims=True)
            acc = jnp.where(adm, acc, -jnp.inf)
        else:
            lo_t = jnp.min(acc, axis=0, keepdims=True)
            hi_t = jnp.max(acc, axis=0, keepdims=True)
        sc_ref[pl.ds(s0, TS), :] = acc
        return lo_t, hi_t

    def p1_body(j, carry):
        lo, hi = carry
        lo_t, hi_t = score_tile(j, False)
        return jnp.minimum(lo, lo_t), jnp.maximum(hi, hi_t)

    lo0 = jnp.full((1, QB), jnp.inf, F32)
    hi0 = jnp.full((1, QB), -jnp.inf, F32)
    lo, hi = lax.fori_loop(0, qi, p1_body, (lo0, hi0))
    lo_t, hi_t = score_tile(qi, True)
    lo = jnp.minimum(lo, lo_t)
    hi = jnp.maximum(hi, hi_t)

    kf = float(topk)
    n_adm = ((q_chunk + 1) * CHUNK).astype(F32)
    need = n_adm > kf

    def count_ge(t):
        def body(j, c):
            s0 = pl.multiple_of(j * TS, TS)
            tile = sc_ref[pl.ds(s0, TS), :]
            return c + jnp.sum(jnp.where(tile >= t, 1.0, 0.0), axis=0, keepdims=True)
        return lax.fori_loop(0, n_tiles, body, jnp.zeros((1, QB), F32))

    def bis_cond(st):
        it, n_open = st[0], st[1]
        return jnp.logical_and(it < BISECT_MAX_ITERS, n_open > 0)

    def bis_body(st):
        it, _, lo, hi, thr, open_ = st
        mid = 0.5 * lo + 0.5 * hi
        c = count_ge(mid)
        found = jnp.logical_and(open_ > 0.0, c == kf)
        thr = jnp.where(found, mid, thr)
        open_ = jnp.where(found, 0.0, open_)
        lo = jnp.where(c > kf, mid, lo)
        hi = jnp.where(c < kf, mid, hi)
        n_open = jnp.sum(open_).astype(jnp.int32)
        return it + 1, n_open, lo, hi, thr, open_

    open0 = jnp.where(need, 1.0, 0.0)
    thr0 = jnp.full((1, QB), jnp.finfo(F32).min, F32)
    st = (jnp.int32(0), jnp.sum(open0).astype(jnp.int32), lo, hi, thr0, open0)
    _, _, lo, hi, thr, open_ = lax.while_loop(bis_cond, bis_body, st)
    thr = jnp.where(open_ > 0.0, lo, thr)

    m_ref[...] = jnp.full(m_ref.shape, NEG_BIG, F32)
    l_ref[...] = jnp.zeros(l_ref.shape, F32)
    acc_ref[...] = jnp.zeros(acc_ref.shape, F32)

    def p3_body(j, carry):
        s0 = pl.multiple_of(j * TS, TS)
        bias_ref[...] = jnp.where(sc_ref[pl.ds(s0, TS), :] >= thr, 0.0, NEG_BIG)
        for h in range(ATT_HEADS):
            hp = h // 2
            kt = k_ref[0, pl.ds(s0, TS), hp * LANES:(hp + 1) * LANES]
            lg = _dot(kt, aqz_ref[h]) + bias_ref[...]
            m_old = m_ref[h:h + 1, :]
            m_new = jnp.maximum(m_old, jnp.max(lg, axis=0, keepdims=True))
            p = jnp.exp(lg - m_new)
            alpha = jnp.exp(m_old - m_new)
            l_ref[h:h + 1, :] = alpha * l_ref[h:h + 1, :] + jnp.sum(p, axis=0, keepdims=True)
            m_ref[h:h + 1, :] = m_new
            vt = vT_ref[0, h * ATT_HEAD_DIM:(h + 1) * ATT_HEAD_DIM, pl.ds(s0, TS)]
            rows = slice(h * ATT_HEAD_DIM, (h + 1) * ATT_HEAD_DIM)
            acc_ref[rows, :] = alpha * acc_ref[rows, :] + _dot(vt, p.astype(BF16))
        return carry

    lax.fori_loop(0, n_tiles, p3_body, 0)

    for h in range(ATT_HEADS):
        rows = slice(h * ATT_HEAD_DIM, (h + 1) * ATT_HEAD_DIM)
        acc_ref[rows, :] = acc_ref[rows, :] * (1.0 / l_ref[h:h + 1, :])
    o_ref[0] = acc_ref[...].T.astype(BF16)


def _dsa(iqT, aqT, wT, kidx, k, vT):
    B, W, S = aqT.shape
    QB, TS = DSA_QB, DSA_TS
    assert S % QB == 0 and QB == TS and QB % CHUNK == 0
    topk = min(TOPK_MAX, S // 4)
    qblk = pl.BlockSpec((1, W, QB), lambda b, q: (b, 0, q))
    return pl.pallas_call(
        functools.partial(_dsa_kernel, topk),
        grid=(B, S // QB),
        in_specs=[
            qblk, qblk,
            pl.BlockSpec((1, IDX_HEADS, QB), lambda b, q: (b, 0, q)),
            pl.BlockSpec((1, S, LANES), lambda b, q: (b, 0, 0)),
            pl.BlockSpec((1, S, W), lambda b, q: (b, 0, 0)),
            pl.BlockSpec((1, W, S), lambda b, q: (b, 0, 0)),
        ],
        out_specs=pl.BlockSpec((1, QB, W), lambda b, q: (b, q, 0)),
        out_shape=jax.ShapeDtypeStruct((B, S, W), BF16),
        scratch_shapes=[
            pltpu.VMEM((S, QB), F32),
            pltpu.VMEM((TS, QB), F32),
            pltpu.VMEM((IDX_HEADS, LANES, QB), BF16),
            pltpu.VMEM((ATT_HEADS, LANES, QB), BF16),
            pltpu.VMEM((ATT_HEADS, QB), F32),
            pltpu.VMEM((ATT_HEADS, QB), F32),
            pltpu.VMEM((W, QB), F32),
        ],
        compiler_params=_cparams(("parallel", "arbitrary")),
        name="dsa",
    )(iqT, aqT, wT, kidx, k, vT)


def _gla_kernel(gq_ref, gk_ref, gv_ref, glr_ref, gr_ref, wgate_ref, bgate_ref, gnorm_ref, tri_ref,
                o_ref, state_ref, obuf_ref):
    T = gq_ref.shape[1]
    t = pl.program_id(1)

    @pl.when(t == 0)
    def _():
        state_ref[...] = jnp.zeros(state_ref.shape, F32)

    gl = _dot(glr_ref[0], wgate_ref[...]) + bgate_ref[...]
    log_a = (jnp.minimum(gl, 0.0) - jnp.log1p(jnp.exp(-jnp.abs(gl)))) * (1.0 / GLA_TAU)
    tri = tri_ref[...]

    for c in range(T // CHUNK):
        r0 = c * CHUNK
        la = log_a[r0:r0 + CHUNK]
        cum = jnp.dot(tri, la, preferred_element_type=F32, precision=lax.Precision.HIGHEST)
        total = cum[CHUNK - 1:CHUNK, :]
        kdec = gk_ref[0, r0:r0 + CHUNK, :].astype(F32) * jnp.exp(total - cum)
        a_tot = jnp.exp(total)
        qc = gq_ref[0, r0:r0 + CHUNK, :]
        for h in range(GLA_HEADS):
            kcols = slice(h * GLA_DK, (h + 1) * GLA_DK)
            vcols = slice(h * GLA_DV, (h + 1) * GLA_DV)
            vh = gv_ref[0, r0:r0 + CHUNK, vcols]
            upd = _dot_tn(vh, kdec[:, kcols].astype(BF16))
            st = a_tot[:, kcols] * state_ref[h] + upd
            state_ref[h] = st
            oh = _dot_nt(qc[:, kcols], st.astype(BF16)) * (GLA_DK ** -0.5)
            obuf_ref[r0:r0 + CHUNK, vcols] = oh

    for h in range(GLA_HEADS):
        vcols = slice(h * GLA_DV, (h + 1) * GLA_DV)
        o = obuf_ref[:, vcols]
        mu = jnp.mean(o, axis=-1, keepdims=True)
        oc = o - mu
        var = jnp.mean(oc * oc, axis=-1, keepdims=True)
        on = oc * lax.rsqrt(var + LN_EPS) * gnorm_ref[:, vcols]
        g = gr_ref[0, :, vcols].astype(F32)
        o_ref[0, :, vcols] = (on * (g * jax.nn.sigmoid(g))).astype(BF16)


def _gla(projg, wgate, bgate, gnorm, tri):
    B, S, _ = projg.shape
    T = min(GLA_TOKENS, S)
    kb = GLA_K_WIDTH
    vb = GLA_V_WIDTH
    full = lambda a: pl.BlockSpec(a.shape, lambda b, t: (0,) * a.ndim)
    return pl.pallas_call(
        _gla_kernel,
        grid=(B, S // T),
        in_specs=[
            pl.BlockSpec((1, T, kb), lambda b, t: (b, t, 0)),
            pl.BlockSpec((1, T, kb), lambda b, t: (b, t, 1)),
            pl.BlockSpec((1, T, vb), lambda b, t: (b, t, 1)),
            pl.BlockSpec((1, T, LANES), lambda b, t: (b, t, 5 * vb // LANES)),
            pl.BlockSpec((1, T, vb), lambda b, t: (b, t, 2)),
            full(wgate), full(bgate), full(gnorm), full(tri),
        ],
        out_specs=pl.BlockSpec((1, T, vb), lambda b, t: (b, t, 0)),
        out_shape=jax.ShapeDtypeStruct((B, S, vb), BF16),
        scratch_shapes=[pltpu.VMEM((GLA_HEADS, GLA_DV, GLA_DK), F32),
                        pltpu.VMEM((T, vb), F32)],
        compiler_params=_cparams(("parallel", "arbitrary")),
        name="gla",
    )(projg, projg, projg, projg, projg, wgate, bgate, gnorm, tri)


def _mix_kernel(x_ref, att_ref, og_ref, ga_ref, gb_ref, pa_ref, pg_ref, wo_ref, bo_ref, g_ref, b_ref, o_ref):
    ya = _dot(att_ref[...], pa_ref[...])
    yb = _dot(og_ref[...], pg_ref[...])
    merged = jax.nn.sigmoid(ga_ref[...].astype(F32)) * ya + jax.nn.sigmoid(gb_ref[...].astype(F32)) * yb
    y = _dot(merged.astype(BF16), wo_ref[...]) + bo_ref[...]
    o_ref[...] = _layer_norm_rows(DN_ALPHA * x_ref[...] + y, g_ref[...], b_ref[...])


def _mix(x2, att2, og2, projg2, pa, pg, wo, bo, g, b):
    N, D = x2.shape
    T = min(MIX_TOKENS, N)
    full = lambda a: pl.BlockSpec(a.shape, lambda i: (0,) * a.ndim)
    return pl.pallas_call(
        _mix_kernel,
        grid=(N // T,),
        in_specs=[
            pl.BlockSpec((T, D), lambda i: (i, 0)),
            pl.BlockSpec((T, ATT_WIDTH), lambda i: (i, 0)),
            pl.BlockSpec((T, GLA_V_WIDTH), lambda i: (i, 0)),
            pl.BlockSpec((T, D), lambda i: (i, 3)),
            pl.BlockSpec((T, D), lambda i: (i, 4)),
            full(pa), full(pg), full(wo), full(bo), full(g), full(b),
        ],
        out_specs=pl.BlockSpec((T, D), lambda i: (i, 0)),
        out_shape=jax.ShapeDtypeStruct((N, D), F32),
        compiler_params=_cparams(("parallel",)),
        name="mix_out",
    )(x2, att2, og2, projg2, projg2, pa, pg, wo, bo, g, b)


def _causal_conv3(u, halo_u, cw, cb):
    T = u.shape[0]
    row = lax.broadcasted_iota(jnp.int32, (T, 1), 0)
    u1 = jnp.where(row == 0, halo_u[7:8, :], pltpu.roll(u, 1, 0))
    r2 = pltpu.roll(u, 2, 0)
    u2 = jnp.where(row == 0, halo_u[6:7, :], jnp.where(row == 1, halo_u[7:8, :], r2))
    return cw[0:1, :] * u2 + cw[1:2, :] * u1 + cw[2:3, :] * u + cb


def _ffn_kernel(x_ref, halo_ref, wa_ref, wb_ref, cwa_ref, cwb_ref, cba_ref, cbb_ref, wd_ref, g_ref, b_ref,
                o_ref, acc_ref):
    t = pl.program_id(1)
    j = pl.program_id(2)
    nj = pl.num_programs(2)
    x = x_ref[0]
    halo = jnp.where(t == 0, 0.0, halo_ref[0])
    xe = jnp.concatenate([halo, x], axis=0).astype(BF16)
    ua = _dot(xe, wa_ref[...])
    ub = _dot(xe, wb_ref[...])
    a = _causal_conv3(ua[SUBLANES:], ua[:SUBLANES], cwa_ref[...], cba_ref[...])
    bb = _causal_conv3(ub[SUBLANES:], ub[:SUBLANES], cwb_ref[...], cbb_ref[...])
    hmid = (jax.nn.gelu(a) * bb).astype(BF16)
    part = _dot(hmid, wd_ref[...])

    @pl.when(j == 0)
    def _():
        acc_ref[...] = part

    @pl.when(j > 0)
    def _():
        acc_ref[...] = acc_ref[...] + part

    @pl.when(j == nj - 1)
    def _():
        o_ref[0] = _layer_norm_rows(DN_ALPHA * x + acc_ref[...], g_ref[...], b_ref[...])


def _ffn(x, wa, wb, cwa, cwb, cba, cbb, wd, g, b, ff_tile):
    B, S, D = x.shape
    T = min(FFN_TOKENS, S)
    F = wa.shape[1]
    nj = F // ff_tile
    hblk = T // SUBLANES
    colblk = lambda rows: pl.BlockSpec((rows, ff_tile), lambda bb_, t, j: (0, j))
    full = lambda a: pl.BlockSpec(a.shape, lambda bb_, t, j: (0,) * a.ndim)
    return pl.pallas_call(
        _ffn_kernel,
        grid=(B, S // T, nj),
        in_specs=[
            pl.BlockSpec((1, T, D), lambda bb_, t, j: (bb_, t, 0)),
            pl.BlockSpec((1, SUBLANES, D), lambda bb_, t, j: (bb_, jnp.maximum(t * hblk - 1, 0), 0)),
            colblk(D), colblk(D),
            colblk(CONV_WIDTH), colblk(CONV_WIDTH), colblk(1), colblk(1),
            pl.BlockSpec((ff_tile, D), lambda bb_, t, j: (j, 0)),
            full(g), full(b),
        ],
        out_specs=pl.BlockSpec((1, T, D), lambda bb_, t, j: (bb_, t, 0)),
        out_shape=jax.ShapeDtypeStruct((B, S, D), F32),
        scratch_shapes=[pltpu.VMEM((T, D), F32)],
        compiler_params=_cparams(("parallel", "parallel", "arbitrary")),
        name="conv_ffn",
    )(x, x, wa, wb, cwa, cwb, cba, cbb, wd, g, b)


def _ff_tile(d_ff):
    for cand in (1408, 1024, 512, 256, 128):
        if d_ff % cand == 0:
            return cand
    return d_ff


def kernel(x, positions, w_in, gla_w_gate, gla_b_gate, gla_norm_g, p_attn, p_gla, w_mix_out, b_mix_out,
           ln1_g, ln1_b, w_up, conv_w, conv_b, w_down, ln2_g, ln2_b):
    B, S, D = x.shape
    depth = w_in.shape[0]
    d_ff = w_down.shape[1]
    pos3 = positions.reshape(B, 1, S)
    inv = (ROPE_THETA ** (-jnp.arange(ROPE_HALF, dtype=F32) * 2.0 / ROPE_DIMS)).reshape(ROPE_HALF, 1)
    tri = jnp.tril(jnp.ones((CHUNK, CHUNK), F32))
    ff_tile = _ff_tile(d_ff)

    o_aq, o_ak, o_av, o_iq = 0, ATT_WIDTH, 2 * ATT_WIDTH, 3 * ATT_WIDTH
    o_ik = 4 * ATT_WIDTH
    o_iw = o_ik + IDX_DIM
    o_gq = o_iw + IDX_HEADS
    o_gk = o_gq + GLA_K_WIDTH
    o_gv = o_gk + GLA_K_WIDTH
    o_glr = o_gv + GLA_V_WIDTH
    o_gr = o_glr + GLA_GATE_RANK
    o_ga = o_gr + GLA_V_WIDTH
    o_gb = o_ga + D
    o_end = o_gb + D

    for l in range(depth):
        w = w_in[l]
        tw = lambda c0, c1: w[:, c0:c1].T.astype(BF16)
        wsm = jnp.concatenate([w[:, o_ik:o_gq].T,
                               jnp.zeros((LANES - IDX_DIM - IDX_HEADS, D), F32)], axis=0).astype(BF16)
        wg = jnp.concatenate([w[:, o_gq:o_glr], w[:, o_gr:o_end], w[:, o_glr:o_gr],
                              jnp.zeros((D, LANES - GLA_GATE_RANK), F32)], axis=1).astype(BF16)
        wgate = jnp.concatenate([gla_w_gate[l], jnp.zeros((LANES - GLA_GATE_RANK, GLA_K_WIDTH), F32)],
                                axis=0).astype(BF16)

        aqT, iqT, vT, k, kidx, wT = _attn_proj(x, pos3, inv, tw(o_aq, o_ak), tw(o_ak, o_av), tw(o_av, o_iq),
                                               tw(o_iq, o_ik), wsm)
        projg = _gla_proj(x, wg)
        att = _dsa(iqT, aqT, wT, kidx, k, vT)
        og = _gla(projg, wgate, gla_b_gate[l].reshape(1, -1), gla_norm_g[l].reshape(1, -1), tri)
        x1 = _mix(x.reshape(B * S, D), att.reshape(B * S, ATT_WIDTH), og.reshape(B * S, GLA_V_WIDTH),
                  projg.reshape(B * S, -1), p_attn[l].astype(BF16), p_gla[l].astype(BF16),
                  w_mix_out[l].astype(BF16), b_mix_out[l].reshape(1, D), ln1_g[l].reshape(1, D),
                  ln1_b[l].reshape(1, D)).reshape(B, S, D)
        wu = w_up[l].astype(BF16)
        x = _ffn(x1, wu[:, :d_ff], wu[:, d_ff:], conv_w[l][:, :d_ff], conv_w[l][:, d_ff:],
                 conv_b[l][:d_ff].reshape(1, d_ff), conv_b[l][d_ff:].reshape(1, d_ff),
                 w_down[l].astype(BF16), ln2_g[l].reshape(1, D), ln2_b[l].reshape(1, D), ff_tile)
    return x
```

```python
import functools

import jax
import jax.numpy as jnp
import numpy as np
from jax import lax
from jax.experimental import pallas as pl
from jax.experimental.pallas import tpu as pltpu

F32 = jnp.float32
BF16 = jnp.bfloat16

DEPTH = 4
CHUNK = 64
ATT_HEADS = 8
ATT_HEAD_DIM = 64
IDX_HEADS = 8
IDX_DIM = 64
TOPK_MAX = 256
ROPE_THETA = 500000.0
ROPE_DIMS = ATT_HEAD_DIM // 4
ROPE_HALF = ROPE_DIMS // 2
GLA_HEADS = 4
GLA_DK = 128
GLA_DV = 256
GLA_GATE_RANK = 16
GLA_TAU = 16.0
CONV_WIDTH = 3
LN_EPS = 1e-5
DN_ALPHA = (2.0 * DEPTH) ** 0.25

ATT_WIDTH = ATT_HEADS * ATT_HEAD_DIM
GLA_K_WIDTH = GLA_HEADS * GLA_DK
GLA_V_WIDTH = GLA_HEADS * GLA_DV

LANES = 128
SUBLANES = 8
VMEM_LIMIT_BYTES = 56 * 1024 * 1024

NEG_BIG = -1e30
LOG2E = 1.4426950408889634

PROJ_TOKENS = 512
DSA_QB = 256
DSA_TS = 256
GLA_TOKENS = 512
MIX_TOKENS = 512
FFN_TOKENS = 512
BISECT_MAX_ITERS = 64


def _cparams(sem):
    return pltpu.CompilerParams(dimension_semantics=sem, vmem_limit_bytes=VMEM_LIMIT_BYTES)


def _dot(a, b):
    return jnp.dot(a, b, preferred_element_type=F32)


def _dot_nt(a, b):
    return lax.dot_general(a, b, (((1,), (1,)), ((), ())), preferred_element_type=F32)


def _dot_tn(a, b):
    return lax.dot_general(a, b, (((0,), (0,)), ((), ())), preferred_element_type=F32)


def _layer_norm_rows(z, g, b):
    mu = jnp.mean(z, axis=-1, keepdims=True)
    zc = z - mu
    var = jnp.mean(zc * zc, axis=-1, keepdims=True)
    return zc * lax.rsqrt(var + LN_EPS) * g + b


def _rope_feature_major(y, cos, sin, heads):
    t = y.shape[-1]
    y3 = y.reshape(heads, ATT_HEAD_DIM, t)
    x1 = y3[:, 0:ROPE_HALF, :]
    x2 = y3[:, ROPE_HALF:ROPE_DIMS, :]
    r1 = x1 * cos[None] - x2 * sin[None]
    r2 = x2 * cos[None] + x1 * sin[None]
    out = jnp.concatenate([r1, r2, y3[:, ROPE_DIMS:, :]], axis=1)
    return out.reshape(heads * ATT_HEAD_DIM, t)


def _attn_proj_kernel(x_ref, pos_ref, inv_ref, wq_ref, wk_ref, wv_ref, wiq_ref, wsm_ref,
                      aqT_ref, iqT_ref, vT_ref, k_ref, kidx_ref, wT_ref):
    xb = x_ref[0].astype(BF16)
    ang = pos_ref[0].astype(F32) * inv_ref[...]
    cos = jnp.cos(ang)
    sin = jnp.sin(ang)

    q = _rope_feature_major(_dot_nt(wq_ref[...], xb), cos, sin, ATT_HEADS)
    aqT_ref[0] = (q * (ATT_HEAD_DIM ** -0.5 * LOG2E)).astype(BF16)
    iq = _rope_feature_major(_dot_nt(wiq_ref[...], xb), cos, sin, IDX_HEADS)
    iqT_ref[0] = (iq * (IDX_DIM ** -0.5)).astype(BF16)
    vT_ref[0] = _dot_nt(wv_ref[...], xb).astype(BF16)
    k = _rope_feature_major(_dot_nt(wk_ref[...], xb), cos, sin, ATT_HEADS)
    k_ref[0] = k.T.astype(BF16)

    sm = _dot_nt(wsm_ref[...], xb)
    ik = _rope_feature_major(sm[0:IDX_DIM], cos, sin, 1)
    row = lax.broadcasted_iota(jnp.int32, sm.shape, 0)
    ik_full = jnp.where(row < IDX_DIM, jnp.concatenate([ik, sm[IDX_DIM:]], axis=0), 0.0)
    kidx_ref[0] = ik_full.T.astype(BF16)
    wT_ref[0] = sm[IDX_DIM:IDX_DIM + IDX_HEADS] * (IDX_HEADS ** -0.5)


def _attn_proj(x, pos3, inv, wq, wk, wv, wiq, wsm):
    B, S, D = x.shape
    T = min(PROJ_TOKENS, S)
    grid = (B, S // T)
    full = lambda a: pl.BlockSpec(a.shape, lambda b, t: (0,) * a.ndim)
    fm = pl.BlockSpec((1, ATT_WIDTH, T), lambda b, t: (b, 0, t))
    out_shape = (
        jax.ShapeDtypeStruct((B, ATT_WIDTH, S), BF16),
        jax.ShapeDtypeStruct((B, ATT_WIDTH, S), BF16),
        jax.ShapeDtypeStruct((B, ATT_WIDTH, S), BF16),
        jax.ShapeDtypeStruct((B, S, ATT_WIDTH), BF16),
        jax.ShapeDtypeStruct((B, S, LANES), BF16),
        jax.ShapeDtypeStruct((B, IDX_HEADS, S), F32),
    )
    return pl.pallas_call(
        _attn_proj_kernel,
        grid=grid,
        in_specs=[
            pl.BlockSpec((1, T, D), lambda b, t: (b, t, 0)),
            pl.BlockSpec((1, 1, T), lambda b, t: (b, 0, t)),
            full(inv), full(wq), full(wk), full(wv), full(wiq), full(wsm),
        ],
        out_specs=(
            fm, fm, fm,
            pl.BlockSpec((1, T, ATT_WIDTH), lambda b, t: (b, t, 0)),
            pl.BlockSpec((1, T, LANES), lambda b, t: (b, t, 0)),
            pl.BlockSpec((1, IDX_HEADS, T), lambda b, t: (b, 0, t)),
        ),
        out_shape=out_shape,
        compiler_params=_cparams(("parallel", "parallel")),
        name="attn_proj",
    )(x, pos3, inv, wq, wk, wv, wiq, wsm)


GLA_PROJ_CHUNK = 1024


def _gla_proj_kernel(x_ref, w_ref, o_ref):
    xb = x_ref[0].astype(BF16)
    width = w_ref.shape[1]
    for c0 in range(0, width, GLA_PROJ_CHUNK):
        c1 = min(c0 + GLA_PROJ_CHUNK, width)
        o_ref[0, :, c0:c1] = _dot(xb, w_ref[:, c0:c1]).astype(BF16)


def _gla_proj(x, wg):
    B, S, D = x.shape
    T = min(PROJ_TOKENS, S)
    W = wg.shape[1]
    return pl.pallas_call(
        _gla_proj_kernel,
        grid=(B, S // T),
        in_specs=[pl.BlockSpec((1, T, D), lambda b, t: (b, t, 0)),
                  pl.BlockSpec(wg.shape, lambda b, t: (0, 0))],
        out_specs=pl.BlockSpec((1, T, W), lambda b, t: (b, t, 0)),
        out_shape=jax.ShapeDtypeStruct((B, S, W), BF16),
        compiler_params=_cparams(("parallel", "parallel")),
        name="gla_proj",
    )(x, wg)


def _sort_key(x):
    u = lax.bitcast_convert_type(x, jnp.int32)
    return u ^ ((u >> 31) & jnp.int32(0x7FFFFFFF))


def _sort_unkey(k):
    return lax.bitcast_convert_type(k ^ ((k >> 31) & jnp.int32(0x7FFFFFFF)), F32)


def _fold8(x):
    return x.reshape(x.shape[0] // SUBLANES, SUBLANES, x.shape[1])


REDUCE_WAYS = 4


def _tree_reduce(x3, op):
    g = x3.shape[0]
    if g > REDUCE_WAYS:
        reducer = {jnp.add: jnp.sum, jnp.maximum: jnp.max, jnp.minimum: jnp.min}[op]
        x3 = reducer(x3.reshape((g // REDUCE_WAYS, REDUCE_WAYS) + x3.shape[1:]), axis=0)
    while x3.shape[0] > 1:
        half = x3.shape[0] // 2
        x3 = op(x3[:half], x3[half:])
    return x3[0]


def _dsa_kernel(topk, iqT_ref, aqT_ref, wT_ref, kidx_ref, k_ref, vT_ref, o_ref,
                sc_ref, bias_ref, lg_ref, iqz_ref, aqz_ref, mrun_ref, macc_ref, l_ref, acc_ref):
    QB = DSA_QB
    TS = DSA_TS
    S = sc_ref.shape[0]
    qi = pl.program_id(1)
    n_tiles = qi + 1
    q0 = qi * QB

    zeros64 = jnp.zeros((IDX_DIM, QB), BF16)
    for h in range(IDX_HEADS):
        iqz_ref[h, 0:IDX_DIM, :] = iqT_ref[0, h * IDX_DIM:(h + 1) * IDX_DIM, :]
        iqz_ref[h, IDX_DIM:, :] = zeros64
    for h in range(ATT_HEADS):
        lo_half = (h % 2) * ATT_HEAD_DIM
        hi_half = (1 - h % 2) * ATT_HEAD_DIM
        aqz_ref[h, lo_half:lo_half + ATT_HEAD_DIM, :] = aqT_ref[0, h * ATT_HEAD_DIM:(h + 1) * ATT_HEAD_DIM, :]
        aqz_ref[h, hi_half:hi_half + ATT_HEAD_DIM, :] = zeros64

    q_chunk = (q0 + lax.broadcasted_iota(jnp.int32, (1, QB), 1)) // CHUNK

    def score_tile(j, diag):
        s0 = pl.multiple_of(j * TS, TS)
        kt = kidx_ref[0, pl.ds(s0, TS), :]
        acc = jnp.zeros((TS, QB), F32)
        for h in range(IDX_HEADS):
            d = _dot(kt, iqz_ref[h])
            acc = acc + wT_ref[0, h:h + 1, :] * jnp.maximum(d, 0.0)
        if diag:
            s_chunk = (s0 + lax.broadcasted_iota(jnp.int32, (TS, 1), 0)) // CHUNK
            adm = s_chunk <= q_chunk
            lo_t = jnp.min(_fold8(jnp.where(adm, acc, jnp.inf)), axis=0)
            hi_t = jnp.max(_fold8(jnp.where(adm, acc, -jnp.inf)), axis=0)
            acc = jnp.where(adm, acc, -jnp.inf)
        else:
            lo_t = jnp.min(_fold8(acc), axis=0)
            hi_t = jnp.max(_fold8(acc), axis=0)
        sc_ref[pl.ds(s0, TS), :] = acc
        return lo_t, hi_t

    def p1_body(j, carry):
        lo8, hi8 = carry
        lo_t, hi_t = score_tile(j, False)
        return jnp.minimum(lo8, lo_t), jnp.maximum(hi8, hi_t)

    lo8 = jnp.full((SUBLANES, QB), jnp.inf, F32)
    hi8 = jnp.full((SUBLANES, QB), -jnp.inf, F32)
    lo8, hi8 = lax.fori_loop(0, qi, p1_body, (lo8, hi8))
    lo_t, hi_t = score_tile(qi, True)
    smin = jnp.min(jnp.minimum(lo8, lo_t), axis=0, keepdims=True)
    smax = jnp.max(jnp.maximum(hi8, hi_t), axis=0, keepdims=True)

    kf = float(topk)
    n_adm = ((q_chunk + 1) * CHUNK).astype(F32)
    need = n_adm > kf

    @pl.when(qi + 1 < S // TS)
    def _():
        sc_ref[pl.ds(pl.multiple_of((qi + 1) * TS, TS), TS), :] = jnp.full((TS, QB), -jnp.inf, F32)

    CT = 2 * TS

    def count_tiles(pred_fn):
        def body(j, c8):
            s0 = pl.multiple_of(j * CT, CT)
            ind = jnp.where(pred_fn(sc_ref[pl.ds(s0, CT), :], s0), 1.0, 0.0)
            return c8 + _tree_reduce(_fold8(ind), jnp.add)
        c8 = lax.fori_loop(0, (n_tiles + 1) // 2, body, jnp.zeros((SUBLANES, QB), F32))
        return jnp.sum(c8, axis=0, keepdims=True)

    def search_cond(st):
        return st[1] > 0

    def search_body(st):
        it, _, lo_i, hi_i, c_lo, c_hi, thr, open_ = st
        collapsed = (hi_i - 1) <= lo_i
        lo_f = _sort_unkey(lo_i)
        hi_f = _sort_unkey(hi_i)
        la = jnp.log(c_lo)
        frac = (la - np.log(kf)) / (la - jnp.log(jnp.maximum(c_hi, 0.5)))
        frac = jnp.clip(frac, 0.02, 0.98)
        guess_i = _sort_key(lo_f + (hi_f - lo_f) * frac)
        half_i = (lo_i >> 1) + (hi_i >> 1) + (lo_i & hi_i & 1)
        mid_i = jnp.where(it % 3 == 2, half_i, guess_i)
        mid_i = jnp.minimum(jnp.maximum(mid_i, lo_i + 1), hi_i - 1)
        t = _sort_unkey(mid_i)
        c = count_tiles(lambda tile, s0: tile >= t)
        live = jnp.logical_and(open_ > 0.0, jnp.logical_not(collapsed))
        found = jnp.logical_and(live, c == kf)
        up = jnp.logical_and(live, c > kf)
        dn = jnp.logical_and(live, c < kf)
        thr = jnp.where(found, t, thr)
        lo_i = jnp.where(up, mid_i, lo_i)
        c_lo = jnp.where(up, c, c_lo)
        hi_i = jnp.where(dn, mid_i, hi_i)
        c_hi = jnp.where(dn, c, c_hi)
        open_ = jnp.where(jnp.logical_or(up, dn), 1.0, 0.0)
        return it + 1, jnp.sum(open_).astype(jnp.int32), lo_i, hi_i, c_lo, c_hi, thr, open_

    open0 = jnp.where(need, 1.0, 0.0)
    thr0 = jnp.full((1, QB), jnp.finfo(F32).min, F32)
    st = (jnp.int32(0), jnp.sum(open0).astype(jnp.int32), _sort_key(smin), _sort_key(smax) + 1,
          n_adm, jnp.zeros((1, QB), F32), thr0, open0)
    _, _, lo_i, hi_i, c_lo, c_hi, thr, _ = lax.while_loop(search_cond, search_body, st)

    tied = jnp.logical_and(need, (hi_i - 1) <= lo_i)
    v = _sort_unkey(lo_i)
    thr = jnp.where(tied, v, thr)

    @pl.when(jnp.sum(jnp.where(tied, 1.0, 0.0)) > 0.0)
    def _():
        want = kf - c_hi

        def idx_body(_, carry):
            j_lo, j_hi = carry
            mid = (j_lo + j_hi) >> 1
            c = count_tiles(lambda tile, s0: jnp.logical_and(
                tile == v, (s0 + lax.broadcasted_iota(jnp.int32, (CT, 1), 0)) <= mid))
            ok = c >= want
            return jnp.where(ok, j_lo, mid), jnp.where(ok, mid, j_hi)

        n_steps = int(np.ceil(np.log2(S))) + 1
        j0 = (jnp.full((1, QB), -1, jnp.int32), jnp.full((1, QB), S - 1, jnp.int32))
        _, j_keep = lax.fori_loop(0, n_steps, idx_body, j0)
        j_keep = jnp.where(tied, j_keep, S)

        def knock_body(j, carry):
            s0 = pl.multiple_of(j * TS, TS)
            tile = sc_ref[pl.ds(s0, TS), :]
            idx = s0 + lax.broadcasted_iota(jnp.int32, (TS, 1), 0)
            drop = jnp.logical_and(jnp.logical_and(tile == v, idx > j_keep), tied)
            sc_ref[pl.ds(s0, TS), :] = jnp.where(drop, -jnp.inf, tile)
            return carry

        lax.fori_loop(0, n_tiles, knock_body, 0)

    mrun_ref[...] = jnp.full(mrun_ref.shape, NEG_BIG, F32)
    macc_ref[...] = jnp.full(macc_ref.shape, NEG_BIG, F32)
    l_ref[...] = jnp.zeros(l_ref.shape, F32)
    acc_ref[...] = jnp.zeros(acc_ref.shape, F32)

    def set_bias(jt, slot):
        s0 = pl.multiple_of(jnp.minimum(jt, qi) * TS, TS)
        thr_eff = jnp.where(jt <= qi, thr, jnp.inf)
        bias_ref[slot] = jnp.where(sc_ref[pl.ds(s0, TS), :] >= thr_eff, 0.0, NEG_BIG)

    def logits_head(jt, slot, h):
        s0 = pl.multiple_of(jnp.minimum(jt, qi) * TS, TS)
        hp = h // 2
        kt = k_ref[0, pl.ds(s0, TS), hp * LANES:(hp + 1) * LANES]
        lg = _dot(kt, aqz_ref[h]) + bias_ref[slot]
        lg_ref[slot, h] = lg
        tile_max = jnp.max(_tree_reduce(_fold8(lg), jnp.maximum), axis=0, keepdims=True)
        return jnp.maximum(mrun_ref[h:h + 1, :], tile_max)

    def value_head(jt, slot, h):
        s0 = pl.multiple_of(jnp.minimum(jt, qi) * TS, TS)
        m_cur = mrun_ref[h:h + 1, :]
        alpha = jnp.exp2(macc_ref[h:h + 1, :] - m_cur)
        macc_ref[h:h + 1, :] = m_cur
        p = jnp.exp2(lg_ref[slot, h] - m_cur)
        l_ref[h:h + 1, :] = alpha * l_ref[h:h + 1, :] + jnp.sum(_tree_reduce(_fold8(p), jnp.add), axis=0,
                                                                 keepdims=True)
        vt = vT_ref[0, h * ATT_HEAD_DIM:(h + 1) * ATT_HEAD_DIM, pl.ds(s0, TS)]
        rows = slice(h * ATT_HEAD_DIM, (h + 1) * ATT_HEAD_DIM)
        acc_ref[rows, :] = alpha * acc_ref[rows, :] + _dot(vt, p.astype(BF16))

    def half_step(j_value, slot_value):
        set_bias(j_value + 1, 1 - slot_value)
        for h in range(ATT_HEADS):
            m_next = logits_head(j_value + 1, 1 - slot_value, h)
            value_head(j_value, slot_value, h)
            mrun_ref[h:h + 1, :] = m_next

    set_bias(0, 0)
    for h in range(ATT_HEADS):
        mrun_ref[h:h + 1, :] = logits_head(0, 0, h)

    def p3_body(i, carry):
        half_step(2 * i, 0)
        half_step(2 * i + 1, 1)
        return carry

    lax.fori_loop(0, (n_tiles + 1) // 2, p3_body, 0)

    for h in range(ATT_HEADS):
        rows = slice(h * ATT_HEAD_DIM, (h + 1) * ATT_HEAD_DIM)
        acc_ref[rows, :] = acc_ref[rows, :] * (1.0 / l_ref[h:h + 1, :])
    o_ref[0] = acc_ref[...].T.astype(BF16)


def _dsa(iqT, aqT, wT, kidx, k, vT):
    B, W, S = aqT.shape
    QB, TS = DSA_QB, DSA_TS
    assert S % QB == 0 and QB == TS and QB % CHUNK == 0
    topk = min(TOPK_MAX, S // 4)
    qblk = pl.BlockSpec((1, W, QB), lambda b, q: (b, 0, q))
    return pl.pallas_call(
        functools.partial(_dsa_kernel, topk),
        grid=(B, S // QB),
        in_specs=[
            qblk, qblk,
            pl.BlockSpec((1, IDX_HEADS, QB), lambda b, q: (b, 0, q)),
            pl.BlockSpec((1, S, LANES), lambda b, q: (b, 0, 0)),
            pl.BlockSpec((1, S, W), lambda b, q: (b, 0, 0)),
            pl.BlockSpec((1, W, S), lambda b, q: (b, 0, 0)),
        ],
        out_specs=pl.BlockSpec((1, QB, W), lambda b, q: (b, q, 0)),
        out_shape=jax.ShapeDtypeStruct((B, S, W), BF16),
        scratch_shapes=[
            pltpu.VMEM((S, QB), F32),
            pltpu.VMEM((2, TS, QB), F32),
            pltpu.VMEM((2, ATT_HEADS, TS, QB), F32),
            pltpu.VMEM((IDX_HEADS, LANES, QB), BF16),
            pltpu.VMEM((ATT_HEADS, LANES, QB), BF16),
            pltpu.VMEM((ATT_HEADS, QB), F32),
            pltpu.VMEM((ATT_HEADS, QB), F32),
            pltpu.VMEM((ATT_HEADS, QB), F32),
            pltpu.VMEM((W, QB), F32),
        ],
        compiler_params=_cparams(("parallel", "arbitrary")),
        name="dsa",
    )(iqT, aqT, wT, kidx, k, vT)


def _gla_kernel(gq_ref, gk_ref, gv_ref, glr_ref, gr_ref, wgate_ref, bgate_ref, gnorm_ref, tri_ref,
                o_ref, state_ref, obuf_ref):
    T = gq_ref.shape[1]
    t = pl.program_id(1)

    @pl.when(t == 0)
    def _():
        state_ref[...] = jnp.zeros(state_ref.shape, F32)

    gl = _dot(glr_ref[0], wgate_ref[...]) + bgate_ref[...]
    log_a = (jnp.minimum(gl, 0.0) - jnp.log1p(jnp.exp(-jnp.abs(gl)))) * (1.0 / GLA_TAU)
    tri = tri_ref[...]

    for c in range(T // CHUNK):
        r0 = c * CHUNK
        la = log_a[r0:r0 + CHUNK]
        cum = jnp.dot(tri, la, preferred_element_type=F32, precision=lax.Precision.HIGHEST)
        total = cum[CHUNK - 1:CHUNK, :]
        kdec = gk_ref[0, r0:r0 + CHUNK, :].astype(F32) * jnp.exp(total - cum)
        a_tot = jnp.exp(total)
        qc = gq_ref[0, r0:r0 + CHUNK, :]
        for h in range(GLA_HEADS):
            kcols = slice(h * GLA_DK, (h + 1) * GLA_DK)
            vcols = slice(h * GLA_DV, (h + 1) * GLA_DV)
            vh = gv_ref[0, r0:r0 + CHUNK, vcols]
            upd = _dot_tn(vh, kdec[:, kcols].astype(BF16))
            st = a_tot[:, kcols] * state_ref[h] + upd
            state_ref[h] = st
            oh = _dot_nt(qc[:, kcols], st.astype(BF16)) * (GLA_DK ** -0.5)
            obuf_ref[r0:r0 + CHUNK, vcols] = oh

    for h in range(GLA_HEADS):
        vcols = slice(h * GLA_DV, (h + 1) * GLA_DV)
        o = obuf_ref[:, vcols]
        mu = jnp.mean(o, axis=-1, keepdims=True)
        oc = o - mu
        var = jnp.mean(oc * oc, axis=-1, keepdims=True)
        on = oc * lax.rsqrt(var + LN_EPS) * gnorm_ref[:, vcols]
        g = gr_ref[0, :, vcols].astype(F32)
        o_ref[0, :, vcols] = (on * (g * jax.nn.sigmoid(g))).astype(BF16)


def _gla(projg, wgate, bgate, gnorm, tri):
    B, S, _ = projg.shape
    T = min(GLA_TOKENS, S)
    kb = GLA_K_WIDTH
    vb = GLA_V_WIDTH
    full = lambda a: pl.BlockSpec(a.shape, lambda b, t: (0,) * a.ndim)
    return pl.pallas_call(
        _gla_kernel,
        grid=(B, S // T),
        in_specs=[
            pl.BlockSpec((1, T, kb), lambda b, t: (b, t, 0)),
            pl.BlockSpec((1, T, kb), lambda b, t: (b, t, 1)),
            pl.BlockSpec((1, T, vb), lambda b, t: (b, t, 1)),
            pl.BlockSpec((1, T, LANES), lambda b, t: (b, t, 5 * vb // LANES)),
            pl.BlockSpec((1, T, vb), lambda b, t: (b, t, 2)),
            full(wgate), full(bgate), full(gnorm), full(tri),
        ],
        out_specs=pl.BlockSpec((1, T, vb), lambda b, t: (b, t, 0)),
        out_shape=jax.ShapeDtypeStruct((B, S, vb), BF16),
        scratch_shapes=[pltpu.VMEM((GLA_HEADS, GLA_DV, GLA_DK), F32),
                        pltpu.VMEM((T, vb), F32)],
        compiler_params=_cparams(("parallel", "arbitrary")),
        name="gla",
    )(projg, projg, projg, projg, projg, wgate, bgate, gnorm, tri)


def _mix_kernel(x_ref, att_ref, og_ref, ga_ref, gb_ref, pa_ref, pg_ref, wo_ref, bo_ref, g_ref, b_ref, o_ref):
    ya = _dot(att_ref[...], pa_ref[...])
    yb = _dot(og_ref[...], pg_ref[...])
    merged = jax.nn.sigmoid(ga_ref[...].astype(F32)) * ya + jax.nn.sigmoid(gb_ref[...].astype(F32)) * yb
    y = _dot(merged.astype(BF16), wo_ref[...]) + bo_ref[...]
    o_ref[...] = _layer_norm_rows(DN_ALPHA * x_ref[...] + y, g_ref[...], b_ref[...])


def _mix(x2, att2, og2, projg2, pa, pg, wo, bo, g, b):
    N, D = x2.shape
    T = min(MIX_TOKENS, N)
    full = lambda a: pl.BlockSpec(a.shape, lambda i: (0,) * a.ndim)
    return pl.pallas_call(
        _mix_kernel,
        grid=(N // T,),
        in_specs=[
            pl.BlockSpec((T, D), lambda i: (i, 0)),
            pl.BlockSpec((T, ATT_WIDTH), lambda i: (i, 0)),
            pl.BlockSpec((T, GLA_V_WIDTH), lambda i: (i, 0)),
            pl.BlockSpec((T, D), lambda i: (i, 3)),
            pl.BlockSpec((T, D), lambda i: (i, 4)),
            full(pa), full(pg), full(wo), full(bo), full(g), full(b),
        ],
        out_specs=pl.BlockSpec((T, D), lambda i: (i, 0)),
        out_shape=jax.ShapeDtypeStruct((N, D), F32),
        compiler_params=_cparams(("parallel",)),
        name="mix_out",
    )(x2, att2, og2, projg2, projg2, pa, pg, wo, bo, g, b)


def _causal_conv3(u, halo_u, cw, cb):
    T = u.shape[0]
    row = lax.broadcasted_iota(jnp.int32, (T, 1), 0)
    u1 = jnp.where(row == 0, halo_u[7:8, :], pltpu.roll(u, 1, 0))
    r2 = pltpu.roll(u, 2, 0)
    u2 = jnp.where(row == 0, halo_u[6:7, :], jnp.where(row == 1, halo_u[7:8, :], r2))
    return cw[0:1, :] * u2 + cw[1:2, :] * u1 + cw[2:3, :] * u + cb


def _ffn_kernel(x_ref, halo_ref, wa_ref, wb_ref, cwa_ref, cwb_ref, cba_ref, cbb_ref, wd_ref, g_ref, b_ref,
                o_ref, acc_ref):
    t = pl.program_id(1)
    j = pl.program_id(2)
    nj = pl.num_programs(2)
    x = x_ref[0]
    halo = jnp.where(t == 0, 0.0, halo_ref[0])
    xe = jnp.concatenate([halo, x], axis=0).astype(BF16)
    ua = _dot(xe, wa_ref[...])
    ub = _dot(xe, wb_ref[...])
    a = _causal_conv3(ua[SUBLANES:], ua[:SUBLANES], cwa_ref[...], cba_ref[...])
    bb = _causal_conv3(ub[SUBLANES:], ub[:SUBLANES], cwb_ref[...], cbb_ref[...])
    hmid = (jax.nn.gelu(a) * bb).astype(BF16)
    part = _dot(hmid, wd_ref[...])

    @pl.when(j == 0)
    def _():
        acc_ref[...] = part

    @pl.when(j > 0)
    def _():
        acc_ref[...] = acc_ref[...] + part

    @pl.when(j == nj - 1)
    def _():
        o_ref[0] = _layer_norm_rows(DN_ALPHA * x + acc_ref[...], g_ref[...], b_ref[...])


def _ffn(x, wa, wb, cwa, cwb, cba, cbb, wd, g, b, ff_tile):
    B, S, D = x.shape
    T = min(FFN_TOKENS, S)
    F = wa.shape[1]
    nj = F // ff_tile
    hblk = T // SUBLANES
    colblk = lambda rows: pl.BlockSpec((rows, ff_tile), lambda bb_, t, j: (0, j))
    full = lambda a: pl.BlockSpec(a.shape, lambda bb_, t, j: (0,) * a.ndim)
    return pl.pallas_call(
        _ffn_kernel,
        grid=(B, S // T, nj),
        in_specs=[
            pl.BlockSpec((1, T, D), lambda bb_, t, j: (bb_, t, 0)),
            pl.BlockSpec((1, SUBLANES, D), lambda bb_, t, j: (bb_, jnp.maximum(t * hblk - 1, 0), 0)),
            colblk(D), colblk(D),
            colblk(CONV_WIDTH), colblk(CONV_WIDTH), colblk(1), colblk(1),
            pl.BlockSpec((ff_tile, D), lambda bb_, t, j: (j, 0)),
            full(g), full(b),
        ],
        out_specs=pl.BlockSpec((1, T, D), lambda bb_, t, j: (bb_, t, 0)),
        out_shape=jax.ShapeDtypeStruct((B, S, D), F32),
        scratch_shapes=[pltpu.VMEM((T, D), F32)],
        compiler_params=_cparams(("parallel", "parallel", "arbitrary")),
        name="conv_ffn",
    )(x, x, wa, wb, cwa, cwb, cba, cbb, wd, g, b)


def _ff_tile(d_ff):
    for cand in (1408, 1024, 512, 256, 128):
        if d_ff % cand == 0:
            return cand
    return d_ff


def kernel(x, positions, w_in, gla_w_gate, gla_b_gate, gla_norm_g, p_attn, p_gla, w_mix_out, b_mix_out,
           ln1_g, ln1_b, w_up, conv_w, conv_b, w_down, ln2_g, ln2_b):
    B, S, D = x.shape
    depth = w_in.shape[0]
    d_ff = w_down.shape[1]
    pos3 = positions.reshape(B, 1, S)
    inv = (ROPE_THETA ** (-jnp.arange(ROPE_HALF, dtype=F32) * 2.0 / ROPE_DIMS)).reshape(ROPE_HALF, 1)
    tri = jnp.tril(jnp.ones((CHUNK, CHUNK), F32))
    ff_tile = _ff_tile(d_ff)

    o_aq, o_ak, o_av, o_iq = 0, ATT_WIDTH, 2 * ATT_WIDTH, 3 * ATT_WIDTH
    o_ik = 4 * ATT_WIDTH
    o_iw = o_ik + IDX_DIM
    o_gq = o_iw + IDX_HEADS
    o_gk = o_gq + GLA_K_WIDTH
    o_gv = o_gk + GLA_K_WIDTH
    o_glr = o_gv + GLA_V_WIDTH
    o_gr = o_glr + GLA_GATE_RANK
    o_ga = o_gr + GLA_V_WIDTH
    o_gb = o_ga + D
    o_end = o_gb + D

    for l in range(depth):
        w = w_in[l]
        tw = lambda c0, c1: w[:, c0:c1].T.astype(BF16)
        wsm = jnp.concatenate([w[:, o_ik:o_gq].T,
                               jnp.zeros((LANES - IDX_DIM - IDX_HEADS, D), F32)], axis=0).astype(BF16)
        wg = jnp.concatenate([w[:, o_gq:o_glr], w[:, o_gr:o_end], w[:, o_glr:o_gr],
                              jnp.zeros((D, LANES - GLA_GATE_RANK), F32)], axis=1).astype(BF16)
        wgate = jnp.concatenate([gla_w_gate[l], jnp.zeros((LANES - GLA_GATE_RANK, GLA_K_WIDTH), F32)],
                                axis=0).astype(BF16)

        aqT, iqT, vT, k, kidx, wT = _attn_proj(x, pos3, inv, tw(o_aq, o_ak), tw(o_ak, o_av), tw(o_av, o_iq),
                                               tw(o_iq, o_ik), wsm)
        projg = _gla_proj(x, wg)
        att = _dsa(iqT, aqT, wT, kidx, k, vT)
        og = _gla(projg, wgate, gla_b_gate[l].reshape(1, -1), gla_norm_g[l].reshape(1, -1), tri)
        x1 = _mix(x.reshape(B * S, D), att.reshape(B * S, ATT_WIDTH), og.reshape(B * S, GLA_V_WIDTH),
                  projg.reshape(B * S, -1), p_attn[l].astype(BF16), p_gla[l].astype(BF16),
                  w_mix_out[l].astype(BF16), b_mix_out[l].reshape(1, D), ln1_g[l].reshape(1, D),
                  ln1_b[l].reshape(1, D)).reshape(B, S, D)
        wu = w_up[l].astype(BF16)
        x = _ffn(x1, wu[:, :d_ff], wu[:, d_ff:], conv_w[l][:, :d_ff], conv_w[l][:, d_ff:],
                 conv_b[l][:d_ff].reshape(1, d_ff), conv_b[l][d_ff:].reshape(1, d_ff),
                 w_down[l].astype(BF16), ln2_g[l].reshape(1, D), ln2_b[l].reshape(1, D), ff_tile)
    return x
```

```python
import functools

import jax
import jax.numpy as jnp
import numpy as np
from jax import lax
from jax.experimental import pallas as pl
from jax.experimental.pallas import tpu as pltpu

F32 = jnp.float32
BF16 = jnp.bfloat16

DEPTH = 4
CHUNK = 64
ATT_HEADS = 8
ATT_HEAD_DIM = 64
IDX_HEADS = 8
IDX_DIM = 64
TOPK_MAX = 256
ROPE_THETA = 500000.0
ROPE_DIMS = ATT_HEAD_DIM // 4
ROPE_HALF = ROPE_DIMS // 2
GLA_HEADS = 4
GLA_DK = 128
GLA_DV = 256
GLA_GATE_RANK = 16
GLA_TAU = 16.0
CONV_WIDTH = 3
LN_EPS = 1e-5
DN_ALPHA = (2.0 * DEPTH) ** 0.25

ATT_WIDTH = ATT_HEADS * ATT_HEAD_DIM
GLA_K_WIDTH = GLA_HEADS * GLA_DK
GLA_V_WIDTH = GLA_HEADS * GLA_DV

LANES = 128
SUBLANES = 8
VMEM_LIMIT_BYTES = 56 * 1024 * 1024

NEG_BIG = -1e30
LOG2E = 1.4426950408889634

PROJ_TOKENS = 512
DSA_QB = 256
DSA_TS = 256
GLA_TOKENS = 512
MIX_TOKENS = 512
FFN_TOKENS = 512
SEARCH_INTERP_ITERS = 18


def _cparams(sem):
    return pltpu.CompilerParams(dimension_semantics=sem, vmem_limit_bytes=VMEM_LIMIT_BYTES)


def _dot(a, b):
    return jnp.dot(a, b, preferred_element_type=F32)


def _dot_nt(a, b):
    return lax.dot_general(a, b, (((1,), (1,)), ((), ())), preferred_element_type=F32)


def _dot_tn(a, b):
    return lax.dot_general(a, b, (((0,), (0,)), ((), ())), preferred_element_type=F32)


def _layer_norm_rows(z, g, b):
    mu = jnp.mean(z, axis=-1, keepdims=True)
    zc = z - mu
    var = jnp.mean(zc * zc, axis=-1, keepdims=True)
    return zc * lax.rsqrt(var + LN_EPS) * g + b


def _rope_feature_major(y, cos, sin, heads):
    t = y.shape[-1]
    y3 = y.reshape(heads, ATT_HEAD_DIM, t)
    x1 = y3[:, 0:ROPE_HALF, :]
    x2 = y3[:, ROPE_HALF:ROPE_DIMS, :]
    r1 = x1 * cos[None] - x2 * sin[None]
    r2 = x2 * cos[None] + x1 * sin[None]
    out = jnp.concatenate([r1, r2, y3[:, ROPE_DIMS:, :]], axis=1)
    return out.reshape(heads * ATT_HEAD_DIM, t)


def _attn_proj_kernel(x_ref, pos_ref, inv_ref, wq_ref, wk_ref, wv_ref, wiq_ref, wsm_ref,
                      aqT_ref, iqT_ref, vT_ref, k_ref, kidx_ref, wT_ref):
    xb = x_ref[0].astype(BF16)
    ang = pos_ref[0].astype(F32) * inv_ref[...]
    cos = jnp.cos(ang)
    sin = jnp.sin(ang)

    q = _rope_feature_major(_dot_nt(wq_ref[...], xb), cos, sin, ATT_HEADS)
    aqT_ref[0] = (q * (ATT_HEAD_DIM ** -0.5 * LOG2E)).astype(BF16)
    iq = _rope_feature_major(_dot_nt(wiq_ref[...], xb), cos, sin, IDX_HEADS)
    iqT_ref[0] = (iq * (IDX_DIM ** -0.5)).astype(BF16)
    vT_ref[0] = _dot_nt(wv_ref[...], xb).astype(BF16)
    k = _rope_feature_major(_dot_nt(wk_ref[...], xb), cos, sin, ATT_HEADS)
    k_ref[0] = k.T.astype(BF16)

    sm = _dot_nt(wsm_ref[...], xb)
    ik = _rope_feature_major(sm[0:IDX_DIM], cos, sin, 1)
    row = lax.broadcasted_iota(jnp.int32, sm.shape, 0)
    ik_full = jnp.where(row < IDX_DIM, jnp.concatenate([ik, sm[IDX_DIM:]], axis=0), 0.0)
    kidx_ref[0] = ik_full.T.astype(BF16)
    wT_ref[0] = sm[IDX_DIM:IDX_DIM + IDX_HEADS] * (IDX_HEADS ** -0.5)


def _attn_proj(x, pos3, inv, wq, wk, wv, wiq, wsm):
    B, S, D = x.shape
    T = min(PROJ_TOKENS, S)
    grid = (B, S // T)
    full = lambda a: pl.BlockSpec(a.shape, lambda b, t: (0,) * a.ndim)
    fm = pl.BlockSpec((1, ATT_WIDTH, T), lambda b, t: (b, 0, t))
    out_shape = (
        jax.ShapeDtypeStruct((B, ATT_WIDTH, S), BF16),
        jax.ShapeDtypeStruct((B, ATT_WIDTH, S), BF16),
        jax.ShapeDtypeStruct((B, ATT_WIDTH, S), BF16),
        jax.ShapeDtypeStruct((B, S, ATT_WIDTH), BF16),
        jax.ShapeDtypeStruct((B, S, LANES), BF16),
        jax.ShapeDtypeStruct((B, IDX_HEADS, S), F32),
    )
    return pl.pallas_call(
        _attn_proj_kernel,
        grid=grid,
        in_specs=[
            pl.BlockSpec((1, T, D), lambda b, t: (b, t, 0)),
            pl.BlockSpec((1, 1, T), lambda b, t: (b, 0, t)),
            full(inv), full(wq), full(wk), full(wv), full(wiq), full(wsm),
        ],
        out_specs=(
            fm, fm, fm,
            pl.BlockSpec((1, T, ATT_WIDTH), lambda b, t: (b, t, 0)),
            pl.BlockSpec((1, T, LANES), lambda b, t: (b, t, 0)),
            pl.BlockSpec((1, IDX_HEADS, T), lambda b, t: (b, 0, t)),
        ),
        out_shape=out_shape,
        compiler_params=_cparams(("parallel", "parallel")),
        name="attn_proj",
    )(x, pos3, inv, wq, wk, wv, wiq, wsm)


GLA_PROJ_CHUNK = 1024


def _gla_proj_kernel(x_ref, w_ref, o_ref):
    xb = x_ref[0].astype(BF16)
    width = w_ref.shape[1]
    for c0 in range(0, width, GLA_PROJ_CHUNK):
        c1 = min(c0 + GLA_PROJ_CHUNK, width)
        o_ref[0, :, c0:c1] = _dot(xb, w_ref[:, c0:c1]).astype(BF16)


def _gla_proj(x, wg):
    B, S, D = x.shape
    T = min(PROJ_TOKENS, S)
    W = wg.shape[1]
    return pl.pallas_call(
        _gla_proj_kernel,
        grid=(B, S // T),
        in_specs=[pl.BlockSpec((1, T, D), lambda b, t: (b, t, 0)),
                  pl.BlockSpec(wg.shape, lambda b, t: (0, 0))],
        out_specs=pl.BlockSpec((1, T, W), lambda b, t: (b, t, 0)),
        out_shape=jax.ShapeDtypeStruct((B, S, W), BF16),
        compiler_params=_cparams(("parallel", "parallel")),
        name="gla_proj",
    )(x, wg)


def _sort_key(x):
    u = lax.bitcast_convert_type(x, jnp.int32)
    return u ^ ((u >> 31) & jnp.int32(0x7FFFFFFF))


def _sort_unkey(k):
    return lax.bitcast_convert_type(k ^ ((k >> 31) & jnp.int32(0x7FFFFFFF)), F32)


def _fold8(x):
    return x.reshape(x.shape[0] // SUBLANES, SUBLANES, x.shape[1])


REDUCE_WAYS = 4


def _tree_reduce(x3, op):
    g = x3.shape[0]
    if g > REDUCE_WAYS:
        reducer = {jnp.add: jnp.sum, jnp.maximum: jnp.max, jnp.minimum: jnp.min}[op]
        x3 = reducer(x3.reshape((g // REDUCE_WAYS, REDUCE_WAYS) + x3.shape[1:]), axis=0)
    while x3.shape[0] > 1:
        half = x3.shape[0] // 2
        x3 = op(x3[:half], x3[half:])
    return x3[0]


def _dsa_kernel(topk, iqT_ref, aqT_ref, wT_ref, kidx_ref, k_ref, vT_ref, o_ref,
                sc_ref, bias_ref, lg_ref, iqz_ref, aqz_ref, mrun_ref, macc_ref, l_ref, acc_ref):
    QB = DSA_QB
    TS = DSA_TS
    S = sc_ref.shape[0]
    qi = pl.program_id(1)
    n_tiles = qi + 1
    q0 = qi * QB

    zeros64 = jnp.zeros((IDX_DIM, QB), BF16)
    for h in range(IDX_HEADS):
        iqz_ref[h, 0:IDX_DIM, :] = iqT_ref[0, h * IDX_DIM:(h + 1) * IDX_DIM, :]
        iqz_ref[h, IDX_DIM:, :] = zeros64
    for h in range(ATT_HEADS):
        lo_half = (h % 2) * ATT_HEAD_DIM
        hi_half = (1 - h % 2) * ATT_HEAD_DIM
        aqz_ref[h, lo_half:lo_half + ATT_HEAD_DIM, :] = aqT_ref[0, h * ATT_HEAD_DIM:(h + 1) * ATT_HEAD_DIM, :]
        aqz_ref[h, hi_half:hi_half + ATT_HEAD_DIM, :] = zeros64

    q_chunk = (q0 + lax.broadcasted_iota(jnp.int32, (1, QB), 1)) // CHUNK

    def score_tile(j, diag):
        s0 = pl.multiple_of(j * TS, TS)
        kt = kidx_ref[0, pl.ds(s0, TS), :]
        acc = jnp.zeros((TS, QB), F32)
        for h in range(IDX_HEADS):
            d = _dot(kt, iqz_ref[h])
            acc = acc + wT_ref[0, h:h + 1, :] * jnp.maximum(d, 0.0)
        if diag:
            s_chunk = (s0 + lax.broadcasted_iota(jnp.int32, (TS, 1), 0)) // CHUNK
            adm = s_chunk <= q_chunk
            lo_t = jnp.min(_fold8(jnp.where(adm, acc, jnp.inf)), axis=0)
            hi_t = jnp.max(_fold8(jnp.where(adm, acc, -jnp.inf)), axis=0)
            acc = jnp.where(adm, acc, -jnp.inf)
        else:
            lo_t = jnp.min(_fold8(acc), axis=0)
            hi_t = jnp.max(_fold8(acc), axis=0)
        sc_ref[pl.ds(s0, TS), :] = acc
        return lo_t, hi_t

    def p1_body(j, carry):
        lo8, hi8 = carry
        lo_t, hi_t = score_tile(j, False)
        return jnp.minimum(lo8, lo_t), jnp.maximum(hi8, hi_t)

    lo8 = jnp.full((SUBLANES, QB), jnp.inf, F32)
    hi8 = jnp.full((SUBLANES, QB), -jnp.inf, F32)
    lo8, hi8 = lax.fori_loop(0, qi, p1_body, (lo8, hi8))
    lo_t, hi_t = score_tile(qi, True)
    smin = jnp.min(jnp.minimum(lo8, lo_t), axis=0, keepdims=True)
    smax = jnp.max(jnp.maximum(hi8, hi_t), axis=0, keepdims=True)

    kf = float(topk)
    n_adm = ((q_chunk + 1) * CHUNK).astype(F32)
    need = n_adm > kf

    @pl.when(qi + 1 < S // TS)
    def _():
        sc_ref[pl.ds(pl.multiple_of((qi + 1) * TS, TS), TS), :] = jnp.full((TS, QB), -jnp.inf, F32)

    CT = 2 * TS

    def count_tiles(pred_fn):
        def body(j, c8):
            s0 = pl.multiple_of(j * CT, CT)
            ind = jnp.where(pred_fn(sc_ref[pl.ds(s0, CT), :], s0), 1.0, 0.0)
            return c8 + _tree_reduce(_fold8(ind), jnp.add)
        c8 = lax.fori_loop(0, (n_tiles + 1) // 2, body, jnp.zeros((SUBLANES, QB), F32))
        return jnp.sum(c8, axis=0, keepdims=True)

    def search_cond(st):
        return st[1] > 0

    def search_body(st):
        it, _, lo_i, hi_i, c_lo, c_hi, thr, open_ = st
        collapsed = (hi_i - 1) <= lo_i
        lo_f = _sort_unkey(lo_i)
        hi_f = _sort_unkey(hi_i)
        la = jnp.log(c_lo)
        frac = (la - np.log(kf)) / (la - jnp.log(jnp.maximum(c_hi, 0.5)))
        frac = jnp.clip(frac, 0.02, 0.98)
        guess_i = _sort_key(lo_f + (hi_f - lo_f) * frac)
        half_i = (lo_i >> 1) + (hi_i >> 1) + (lo_i & hi_i & 1)
        mid_i = jnp.where(jnp.logical_or(it % 3 == 2, it >= SEARCH_INTERP_ITERS), half_i, guess_i)
        mid_i = jnp.minimum(jnp.maximum(mid_i, lo_i + 1), hi_i - 1)
        t = _sort_unkey(mid_i)
        c = count_tiles(lambda tile, s0: tile >= t)
        live = jnp.logical_and(open_ > 0.0, jnp.logical_not(collapsed))
        found = jnp.logical_and(live, c == kf)
        up = jnp.logical_and(live, c > kf)
        dn = jnp.logical_and(live, c < kf)
        thr = jnp.where(found, t, thr)
        lo_i = jnp.where(up, mid_i, lo_i)
        c_lo = jnp.where(up, c, c_lo)
        hi_i = jnp.where(dn, mid_i, hi_i)
        c_hi = jnp.where(dn, c, c_hi)
        open_ = jnp.where(jnp.logical_or(up, dn), 1.0, 0.0)
        return it + 1, jnp.sum(open_).astype(jnp.int32), lo_i, hi_i, c_lo, c_hi, thr, open_

    zero_f = jnp.zeros((1, QB), F32)
    tiny_f = jnp.full((1, QB), jnp.finfo(F32).tiny, F32)
    c_zero = count_tiles(lambda tile, s0: tile >= zero_f)
    c_tiny = count_tiles(lambda tile, s0: tile >= tiny_f)
    key_zero = _sort_key(zero_f)
    key_tiny = _sort_key(tiny_f)
    pos_side = c_tiny > kf
    neg_side = c_zero < kf
    zero_tie = jnp.logical_and(c_zero > kf, c_tiny < kf)
    lo_i0 = jnp.where(pos_side, key_tiny, jnp.where(zero_tie, key_zero, _sort_key(smin)))
    c_lo0 = jnp.where(pos_side, c_tiny, jnp.where(zero_tie, c_zero, n_adm))
    hi_i0 = jnp.where(neg_side, key_zero, jnp.where(zero_tie, key_zero + 1, _sort_key(smax) + 1))
    c_hi0 = jnp.where(neg_side, c_zero, jnp.where(zero_tie, c_tiny, 0.0))
    thr0 = jnp.full((1, QB), jnp.finfo(F32).min, F32)
    thr0 = jnp.where(jnp.logical_and(need, c_zero == kf), zero_f, thr0)
    thr0 = jnp.where(jnp.logical_and(need, c_tiny == kf), tiny_f, thr0)
    open0 = jnp.where(jnp.logical_and(need, jnp.logical_or(pos_side, neg_side)), 1.0, 0.0)
    st = (jnp.int32(0), jnp.sum(open0).astype(jnp.int32), lo_i0, hi_i0, c_lo0, c_hi0, thr0, open0)
    _, _, lo_i, hi_i, c_lo, c_hi, thr, _ = lax.while_loop(search_cond, search_body, st)

    unresolved = thr == jnp.finfo(F32).min
    tied = jnp.logical_and(jnp.logical_and(need, unresolved), (hi_i - 1) <= lo_i)
    v = _sort_unkey(lo_i)
    thr = jnp.where(tied, v, thr)

    @pl.when(jnp.sum(jnp.where(tied, 1.0, 0.0)) > 0.0)
    def _():
        want = kf - c_hi

        def idx_cond(st):
            return st[1] > 0

        def idx_body(st):
            it, _, j_lo, j_hi, c_jl, c_jh, j_keep, open_ = st
            frac = (want - c_jl) / jnp.maximum(c_jh - c_jl, 1.0)
            guess = j_lo + ((j_hi - j_lo).astype(F32) * frac).astype(jnp.int32)
            mid = jnp.where(it % 3 == 2, (j_lo + j_hi) >> 1, guess)
            mid = jnp.minimum(jnp.maximum(mid, j_lo + 1), j_hi - 1)
            c = count_tiles(lambda tile, s0: jnp.logical_and(
                tile == v, (s0 + lax.broadcasted_iota(jnp.int32, (CT, 1), 0)) <= mid))
            live = open_ > 0.0
            below = jnp.logical_and(live, c < want)
            above = jnp.logical_and(live, c > want)
            j_keep = jnp.where(jnp.logical_and(live, c >= want), mid, j_keep)
            j_lo = jnp.where(below, mid, j_lo)
            c_jl = jnp.where(below, c, c_jl)
            j_hi = jnp.where(above, mid, j_hi)
            c_jh = jnp.where(above, c, c_jh)
            still = jnp.logical_and(jnp.logical_or(below, above), (j_hi - j_lo) > 1)
            open_ = jnp.where(still, 1.0, 0.0)
            return it + 1, jnp.sum(open_).astype(jnp.int32), j_lo, j_hi, c_jl, c_jh, j_keep, open_

        open_t = jnp.where(tied, 1.0, 0.0)
        st_t = (jnp.int32(0), jnp.sum(open_t).astype(jnp.int32),
                jnp.full((1, QB), -1, jnp.int32), jnp.full((1, QB), S - 1, jnp.int32),
                jnp.zeros((1, QB), F32), c_lo - c_hi, jnp.full((1, QB), S - 1, jnp.int32), open_t)
        j_keep = lax.while_loop(idx_cond, idx_body, st_t)[6]
        j_keep = jnp.where(tied, j_keep, S)

        def knock_body(j, carry):
            s0 = pl.multiple_of(j * TS, TS)
            tile = sc_ref[pl.ds(s0, TS), :]
            idx = s0 + lax.broadcasted_iota(jnp.int32, (TS, 1), 0)
            drop = jnp.logical_and(jnp.logical_and(tile == v, idx > j_keep), tied)
            sc_ref[pl.ds(s0, TS), :] = jnp.where(drop, -jnp.inf, tile)
            return carry

        lax.fori_loop(0, n_tiles, knock_body, 0)

    mrun_ref[...] = jnp.full(mrun_ref.shape, NEG_BIG, F32)
    macc_ref[...] = jnp.full(macc_ref.shape, NEG_BIG, F32)
    l_ref[...] = jnp.zeros(l_ref.shape, F32)
    acc_ref[...] = jnp.zeros(acc_ref.shape, F32)

    def set_bias(jt, slot):
        s0 = pl.multiple_of(jnp.minimum(jt, qi) * TS, TS)
        thr_eff = jnp.where(jt <= qi, thr, jnp.inf)
        bias_ref[slot] = jnp.where(sc_ref[pl.ds(s0, TS), :] >= thr_eff, 0.0, NEG_BIG)

    def logits_head(jt, slot, h):
        s0 = pl.multiple_of(jnp.minimum(jt, qi) * TS, TS)
        hp = h // 2
        kt = k_ref[0, pl.ds(s0, TS), hp * LANES:(hp + 1) * LANES]
        lg = _dot(kt, aqz_ref[h]) + bias_ref[slot]
        lg_ref[slot, h] = lg
        tile_max = jnp.max(_tree_reduce(_fold8(lg), jnp.maximum), axis=0, keepdims=True)
        return jnp.maximum(mrun_ref[h:h + 1, :], tile_max)

    def value_head(jt, slot, h):
        s0 = pl.multiple_of(jnp.minimum(jt, qi) * TS, TS)
        m_cur = mrun_ref[h:h + 1, :]
        alpha = jnp.exp2(macc_ref[h:h + 1, :] - m_cur)
        macc_ref[h:h + 1, :] = m_cur
        p = jnp.exp2(lg_ref[slot, h] - m_cur)
        l_ref[h:h + 1, :] = alpha * l_ref[h:h + 1, :] + jnp.sum(_tree_reduce(_fold8(p), jnp.add), axis=0,
                                                                 keepdims=True)
        vt = vT_ref[0, h * ATT_HEAD_DIM:(h + 1) * ATT_HEAD_DIM, pl.ds(s0, TS)]
        rows = slice(h * ATT_HEAD_DIM, (h + 1) * ATT_HEAD_DIM)
        acc_ref[rows, :] = alpha * acc_ref[rows, :] + _dot(vt, p.astype(BF16))

    def half_step(j_value, slot_value):
        set_bias(j_value + 1, 1 - slot_value)
        for h in range(ATT_HEADS):
            m_next = logits_head(j_value + 1, 1 - slot_value, h)
            value_head(j_value, slot_value, h)
            mrun_ref[h:h + 1, :] = m_next

    set_bias(0, 0)
    for h in range(ATT_HEADS):
        mrun_ref[h:h + 1, :] = logits_head(0, 0, h)

    def p3_body(i, carry):
        half_step(2 * i, 0)
        half_step(2 * i + 1, 1)
        return carry

    lax.fori_loop(0, (n_tiles + 1) // 2, p3_body, 0)

    for h in range(ATT_HEADS):
        rows = slice(h * ATT_HEAD_DIM, (h + 1) * ATT_HEAD_DIM)
        acc_ref[rows, :] = acc_ref[rows, :] * (1.0 / l_ref[h:h + 1, :])
    o_ref[0] = acc_ref[...].T.astype(BF16)


def _dsa(iqT, aqT, wT, kidx, k, vT):
    B, W, S = aqT.shape
    QB, TS = DSA_QB, DSA_TS
    assert S % QB == 0 and QB == TS and QB % CHUNK == 0
    topk = min(TOPK_MAX, S // 4)
    qblk = pl.BlockSpec((1, W, QB), lambda b, q: (b, 0, q))
    return pl.pallas_call(
        functools.partial(_dsa_kernel, topk),
        grid=(B, S // QB),
        in_specs=[
            qblk, qblk,
            pl.BlockSpec((1, IDX_HEADS, QB), lambda b, q: (b, 0, q)),
            pl.BlockSpec((1, S, LANES), lambda b, q: (b, 0, 0)),
            pl.BlockSpec((1, S, W), lambda b, q: (b, 0, 0)),
            pl.BlockSpec((1, W, S), lambda b, q: (b, 0, 0)),
        ],
        out_specs=pl.BlockSpec((1, QB, W), lambda b, q: (b, q, 0)),
        out_shape=jax.ShapeDtypeStruct((B, S, W), BF16),
        scratch_shapes=[
            pltpu.VMEM((S, QB), F32),
            pltpu.VMEM((2, TS, QB), F32),
            pltpu.VMEM((2, ATT_HEADS, TS, QB), F32),
            pltpu.VMEM((IDX_HEADS, LANES, QB), BF16),
            pltpu.VMEM((ATT_HEADS, LANES, QB), BF16),
            pltpu.VMEM((ATT_HEADS, QB), F32),
            pltpu.VMEM((ATT_HEADS, QB), F32),
            pltpu.VMEM((ATT_HEADS, QB), F32),
            pltpu.VMEM((W, QB), F32),
        ],
        compiler_params=_cparams(("parallel", "arbitrary")),
        name="dsa",
    )(iqT, aqT, wT, kidx, k, vT)


def _gla_kernel(gq_ref, gk_ref, gv_ref, glr_ref, gr_ref, wgate_ref, bgate_ref, gnorm_ref, tri_ref,
                o_ref, state_ref, obuf_ref):
    T = gq_ref.shape[1]
    t = pl.program_id(1)

    @pl.when(t == 0)
    def _():
        state_ref[...] = jnp.zeros(state_ref.shape, F32)

    gl = _dot(glr_ref[0], wgate_ref[...]) + bgate_ref[...]
    log_a = (jnp.minimum(gl, 0.0) - jnp.log1p(jnp.exp(-jnp.abs(gl)))) * (1.0 / GLA_TAU)
    tri = tri_ref[...]

    for c in range(T // CHUNK):
        r0 = c * CHUNK
        la = log_a[r0:r0 + CHUNK]
        cum = jnp.dot(tri, la, preferred_element_type=F32, precision=lax.Precision.HIGHEST)
        total = cum[CHUNK - 1:CHUNK, :]
        kdec = gk_ref[0, r0:r0 + CHUNK, :].astype(F32) * jnp.exp(total - cum)
        a_tot = jnp.exp(total)
        qc = gq_ref[0, r0:r0 + CHUNK, :]
        for h in range(GLA_HEADS):
            kcols = slice(h * GLA_DK, (h + 1) * GLA_DK)
            vcols = slice(h * GLA_DV, (h + 1) * GLA_DV)
            vh = gv_ref[0, r0:r0 + CHUNK, vcols]
            upd = _dot_tn(vh, kdec[:, kcols].astype(BF16))
            st = a_tot[:, kcols] * state_ref[h] + upd
            state_ref[h] = st
            oh = _dot_nt(qc[:, kcols], st.astype(BF16)) * (GLA_DK ** -0.5)
            obuf_ref[r0:r0 + CHUNK, vcols] = oh

    for h in range(GLA_HEADS):
        vcols = slice(h * GLA_DV, (h + 1) * GLA_DV)
        o = obuf_ref[:, vcols]
        mu = jnp.mean(o, axis=-1, keepdims=True)
        oc = o - mu
        var = jnp.mean(oc * oc, axis=-1, keepdims=True)
        on = oc * lax.rsqrt(var + LN_EPS) * gnorm_ref[:, vcols]
        g = gr_ref[0, :, vcols].astype(F32)
        o_ref[0, :, vcols] = (on * (g * jax.nn.sigmoid(g))).astype(BF16)


def _gla(projg, wgate, bgate, gnorm, tri):
    B, S, _ = projg.shape
    T = min(GLA_TOKENS, S)
    kb = GLA_K_WIDTH
    vb = GLA_V_WIDTH
    full = lambda a: pl.BlockSpec(a.shape, lambda b, t: (0,) * a.ndim)
    return pl.pallas_call(
        _gla_kernel,
        grid=(B, S // T),
        in_specs=[
            pl.BlockSpec((1, T, kb), lambda b, t: (b, t, 0)),
            pl.BlockSpec((1, T, kb), lambda b, t: (b, t, 1)),
            pl.BlockSpec((1, T, vb), lambda b, t: (b, t, 1)),
            pl.BlockSpec((1, T, LANES), lambda b, t: (b, t, 5 * vb // LANES)),
            pl.BlockSpec((1, T, vb), lambda b, t: (b, t, 2)),
            full(wgate), full(bgate), full(gnorm), full(tri),
        ],
        out_specs=pl.BlockSpec((1, T, vb), lambda b, t: (b, t, 0)),
        out_shape=jax.ShapeDtypeStruct((B, S, vb), BF16),
        scratch_shapes=[pltpu.VMEM((GLA_HEADS, GLA_DV, GLA_DK), F32),
                        pltpu.VMEM((T, vb), F32)],
        compiler_params=_cparams(("parallel", "arbitrary")),
        name="gla",
    )(projg, projg, projg, projg, projg, wgate, bgate, gnorm, tri)


def _mix_kernel(x_ref, att_ref, og_ref, ga_ref, gb_ref, pa_ref, pg_ref, wo_ref, bo_ref, g_ref, b_ref, o_ref):
    ya = _dot(att_ref[...], pa_ref[...])
    yb = _dot(og_ref[...], pg_ref[...])
    merged = jax.nn.sigmoid(ga_ref[...].astype(F32)) * ya + jax.nn.sigmoid(gb_ref[...].astype(F32)) * yb
    y = _dot(merged.astype(BF16), wo_ref[...]) + bo_ref[...]
    o_ref[...] = _layer_norm_rows(DN_ALPHA * x_ref[...] + y, g_ref[...], b_ref[...])


def _mix(x2, att2, og2, projg2, pa, pg, wo, bo, g, b):
    N, D = x2.shape
    T = min(MIX_TOKENS, N)
    full = lambda a: pl.BlockSpec(a.shape, lambda i: (0,) * a.ndim)
    return pl.pallas_call(
        _mix_kernel,
        grid=(N // T,),
        in_specs=[
            pl.BlockSpec((T, D), lambda i: (i, 0)),
            pl.BlockSpec((T, ATT_WIDTH), lambda i: (i, 0)),
            pl.BlockSpec((T, GLA_V_WIDTH), lambda i: (i, 0)),
            pl.BlockSpec((T, D), lambda i: (i, 3)),
            pl.BlockSpec((T, D), lambda i: (i, 4)),
            full(pa), full(pg), full(wo), full(bo), full(g), full(b),
        ],
        out_specs=pl.BlockSpec((T, D), lambda i: (i, 0)),
        out_shape=jax.ShapeDtypeStruct((N, D), F32),
        compiler_params=_cparams(("parallel",)),
        name="mix_out",
    )(x2, att2, og2, projg2, projg2, pa, pg, wo, bo, g, b)


def _causal_conv3(u, halo_u, cw, cb):
    T = u.shape[0]
    row = lax.broadcasted_iota(jnp.int32, (T, 1), 0)
    u1 = jnp.where(row == 0, halo_u[7:8, :], pltpu.roll(u, 1, 0))
    r2 = pltpu.roll(u, 2, 0)
    u2 = jnp.where(row == 0, halo_u[6:7, :], jnp.where(row == 1, halo_u[7:8, :], r2))
    return cw[0:1, :] * u2 + cw[1:2, :] * u1 + cw[2:3, :] * u + cb


def _ffn_kernel(x_ref, halo_ref, wa_ref, wb_ref, cwa_ref, cwb_ref, cba_ref, cbb_ref, wd_ref, g_ref, b_ref,
                o_ref, acc_ref):
    t = pl.program_id(1)
    j = pl.program_id(2)
    nj = pl.num_programs(2)
    x = x_ref[0]
    halo = jnp.where(t == 0, 0.0, halo_ref[0])
    xe = jnp.concatenate([halo, x], axis=0).astype(BF16)
    ua = _dot(xe, wa_ref[...])
    ub = _dot(xe, wb_ref[...])
    a = _causal_conv3(ua[SUBLANES:], ua[:SUBLANES], cwa_ref[...], cba_ref[...])
    bb = _causal_conv3(ub[SUBLANES:], ub[:SUBLANES], cwb_ref[...], cbb_ref[...])
    hmid = (jax.nn.gelu(a) * bb).astype(BF16)
    part = _dot(hmid, wd_ref[...])

    @pl.when(j == 0)
    def _():
        acc_ref[...] = part

    @pl.when(j > 0)
    def _():
        acc_ref[...] = acc_ref[...] + part

    @pl.when(j == nj - 1)
    def _():
        o_ref[0] = _layer_norm_rows(DN_ALPHA * x + acc_ref[...], g_ref[...], b_ref[...])


def _ffn(x, wa, wb, cwa, cwb, cba, cbb, wd, g, b, ff_tile):
    B, S, D = x.shape
    T = min(FFN_TOKENS, S)
    F = wa.shape[1]
    nj = F // ff_tile
    hblk = T // SUBLANES
    colblk = lambda rows: pl.BlockSpec((rows, ff_tile), lambda bb_, t, j: (0, j))
    full = lambda a: pl.BlockSpec(a.shape, lambda bb_, t, j: (0,) * a.ndim)
    return pl.pallas_call(
        _ffn_kernel,
        grid=(B, S // T, nj),
        in_specs=[
            pl.BlockSpec((1, T, D), lambda bb_, t, j: (bb_, t, 0)),
            pl.BlockSpec((1, SUBLANES, D), lambda bb_, t, j: (bb_, jnp.maximum(t * hblk - 1, 0), 0)),
            colblk(D), colblk(D),
            colblk(CONV_WIDTH), colblk(CONV_WIDTH), colblk(1), colblk(1),
            pl.BlockSpec((ff_tile, D), lambda bb_, t, j: (j, 0)),
            full(g), full(b),
        ],
        out_specs=pl.BlockSpec((1, T, D), lambda bb_, t, j: (bb_, t, 0)),
        out_shape=jax.ShapeDtypeStruct((B, S, D), F32),
        scratch_shapes=[pltpu.VMEM((T, D), F32)],
        compiler_params=_cparams(("parallel", "parallel", "arbitrary")),
        name="conv_ffn",
    )(x, x, wa, wb, cwa, cwb, cba, cbb, wd, g, b)


def _ff_tile(d_ff):
    for cand in (1408, 1024, 512, 256, 128):
        if d_ff % cand == 0:
            return cand
    return d_ff


def kernel(x, positions, w_in, gla_w_gate, gla_b_gate, gla_norm_g, p_attn, p_gla, w_mix_out, b_mix_out,
           ln1_g, ln1_b, w_up, conv_w, conv_b, w_down, ln2_g, ln2_b):
    B, S, D = x.shape
    depth = w_in.shape[0]
    d_ff = w_down.shape[1]
    pos3 = positions.reshape(B, 1, S)
    inv = (ROPE_THETA ** (-jnp.arange(ROPE_HALF, dtype=F32) * 2.0 / ROPE_DIMS)).reshape(ROPE_HALF, 1)
    tri = jnp.tril(jnp.ones((CHUNK, CHUNK), F32))
    ff_tile = _ff_tile(d_ff)

    o_aq, o_ak, o_av, o_iq = 0, ATT_WIDTH, 2 * ATT_WIDTH, 3 * ATT_WIDTH
    o_ik = 4 * ATT_WIDTH
    o_iw = o_ik + IDX_DIM
    o_gq = o_iw + IDX_HEADS
    o_gk = o_gq + GLA_K_WIDTH
    o_gv = o_gk + GLA_K_WIDTH
    o_glr = o_gv + GLA_V_WIDTH
    o_gr = o_glr + GLA_GATE_RANK
    o_ga = o_gr + GLA_V_WIDTH
    o_gb = o_ga + D
    o_end = o_gb + D

    for l in range(depth):
        w = w_in[l]
        tw = lambda c0, c1: w[:, c0:c1].T.astype(BF16)
        wsm = jnp.concatenate([w[:, o_ik:o_gq].T,
                               jnp.zeros((LANES - IDX_DIM - IDX_HEADS, D), F32)], axis=0).astype(BF16)
        wg = jnp.concatenate([w[:, o_gq:o_glr], w[:, o_gr:o_end], w[:, o_glr:o_gr],
                              jnp.zeros((D, LANES - GLA_GATE_RANK), F32)], axis=1).astype(BF16)
        wgate = jnp.concatenate([gla_w_gate[l], jnp.zeros((LANES - GLA_GATE_RANK, GLA_K_WIDTH), F32)],
                                axis=0).astype(BF16)

        aqT, iqT, vT, k, kidx, wT = _attn_proj(x, pos3, inv, tw(o_aq, o_ak), tw(o_ak, o_av), tw(o_av, o_iq),
                                               tw(o_iq, o_ik), wsm)
        projg = _gla_proj(x, wg)
        att = _dsa(iqT, aqT, wT, kidx, k, vT)
        og = _gla(projg, wgate, gla_b_gate[l].reshape(1, -1), gla_norm_g[l].reshape(1, -1), tri)
        x1 = _mix(x.reshape(B * S, D), att.reshape(B * S, ATT_WIDTH), og.reshape(B * S, GLA_V_WIDTH),
                  projg.reshape(B * S, -1), p_attn[l].astype(BF16), p_gla[l].astype(BF16),
                  w_mix_out[l].astype(BF16), b_mix_out[l].reshape(1, D), ln1_g[l].reshape(1, D),
                  ln1_b[l].reshape(1, D)).reshape(B, S, D)
        wu = w_up[l].astype(BF16)
        x = _ffn(x1, wu[:, :d_ff], wu[:, d_ff:], conv_w[l][:, :d_ff], conv_w[l][:, d_ff:],
                 conv_b[l][:d_ff].reshape(1, d_ff), conv_b[l][d_ff:].reshape(1, d_ff),
                 w_down[l].astype(BF16), ln2_g[l].reshape(1, D), ln2_b[l].reshape(1, D), ff_tile)
    return x
```

```python
import functools

import jax
import jax.numpy as jnp
import numpy as np
from jax import lax
from jax.experimental import pallas as pl
from jax.experimental.pallas import tpu as pltpu

F32 = jnp.float32
BF16 = jnp.bfloat16

DEPTH = 4
CHUNK = 64
ATT_HEADS = 8
ATT_HEAD_DIM = 64
IDX_HEADS = 8
IDX_DIM = 64
TOPK_MAX = 256
ROPE_THETA = 500000.0
ROPE_DIMS = ATT_HEAD_DIM // 4
ROPE_HALF = ROPE_DIMS // 2
GLA_HEADS = 4
GLA_DK = 128
GLA_DV = 256
GLA_GATE_RANK = 16
GLA_TAU = 16.0
CONV_WIDTH = 3
LN_EPS = 1e-5
DN_ALPHA = (2.0 * DEPTH) ** 0.25

ATT_WIDTH = ATT_HEADS * ATT_HEAD_DIM
GLA_K_WIDTH = GLA_HEADS * GLA_DK
GLA_V_WIDTH = GLA_HEADS * GLA_DV

LANES = 128
SUBLANES = 8
VMEM_LIMIT_BYTES = 56 * 1024 * 1024

NEG_BIG = -1e30
LOG2E = 1.4426950408889634

PROJ_TOKENS = 512
DSA_QB = 256
DSA_TS = 256
GLA_TOKENS = 512
MIX_TOKENS = 512
FFN_TOKENS = 512
SEARCH_INTERP_ITERS = 18


def _cparams(sem):
    return pltpu.CompilerParams(dimension_semantics=sem, vmem_limit_bytes=VMEM_LIMIT_BYTES)


def _dot(a, b):
    return jnp.dot(a, b, preferred_element_type=F32)


def _dot_nt(a, b):
    return lax.dot_general(a, b, (((1,), (1,)), ((), ())), preferred_element_type=F32)


def _dot_tn(a, b):
    return lax.dot_general(a, b, (((0,), (0,)), ((), ())), preferred_element_type=F32)


def _layer_norm_rows(z, g, b):
    mu = jnp.mean(z, axis=-1, keepdims=True)
    zc = z - mu
    var = jnp.mean(zc * zc, axis=-1, keepdims=True)
    return zc * lax.rsqrt(var + LN_EPS) * g + b


def _rope_feature_major(y, cos, sin, heads):
    t = y.shape[-1]
    y3 = y.reshape(heads, ATT_HEAD_DIM, t)
    x1 = y3[:, 0:ROPE_HALF, :]
    x2 = y3[:, ROPE_HALF:ROPE_DIMS, :]
    r1 = x1 * cos[None] - x2 * sin[None]
    r2 = x2 * cos[None] + x1 * sin[None]
    out = jnp.concatenate([r1, r2, y3[:, ROPE_DIMS:, :]], axis=1)
    return out.reshape(heads * ATT_HEAD_DIM, t)


def _attn_proj_kernel(x_ref, pos_ref, inv_ref, wq_ref, wk_ref, wv_ref, wiq_ref, wsm_ref,
                      aqT_ref, iqT_ref, vT_ref, k_ref, kidx_ref, wT_ref):
    xb = x_ref[0].astype(BF16)
    ang = pos_ref[0].astype(F32) * inv_ref[...]
    cos = jnp.cos(ang)
    sin = jnp.sin(ang)

    q = _rope_feature_major(_dot_nt(wq_ref[...], xb), cos, sin, ATT_HEADS)
    aqT_ref[0] = (q * (ATT_HEAD_DIM ** -0.5 * LOG2E)).astype(BF16)
    iq = _rope_feature_major(_dot_nt(wiq_ref[...], xb), cos, sin, IDX_HEADS)
    iqT_ref[0] = (iq * (IDX_DIM ** -0.5)).astype(BF16)
    vT_ref[0] = _dot_nt(wv_ref[...], xb).astype(BF16)
    k = _rope_feature_major(_dot_nt(wk_ref[...], xb), cos, sin, ATT_HEADS)
    k_ref[0] = k.T.astype(BF16)

    sm = _dot_nt(wsm_ref[...], xb)
    ik = _rope_feature_major(sm[0:IDX_DIM], cos, sin, 1)
    row = lax.broadcasted_iota(jnp.int32, sm.shape, 0)
    ik_full = jnp.where(row < IDX_DIM, jnp.concatenate([ik, sm[IDX_DIM:]], axis=0), 0.0)
    kidx_ref[0] = ik_full.T.astype(BF16)
    wT_ref[0] = sm[IDX_DIM:IDX_DIM + IDX_HEADS] * (IDX_HEADS ** -0.5)


def _attn_proj(x, pos3, inv, wq, wk, wv, wiq, wsm):
    B, S, D = x.shape
    T = min(PROJ_TOKENS, S)
    grid = (B, S // T)
    full = lambda a: pl.BlockSpec(a.shape, lambda b, t: (0,) * a.ndim)
    fm = pl.BlockSpec((1, ATT_WIDTH, T), lambda b, t: (b, 0, t))
    out_shape = (
        jax.ShapeDtypeStruct((B, ATT_WIDTH, S), BF16),
        jax.ShapeDtypeStruct((B, ATT_WIDTH, S), BF16),
        jax.ShapeDtypeStruct((B, ATT_WIDTH, S), BF16),
        jax.ShapeDtypeStruct((B, S, ATT_WIDTH), BF16),
        jax.ShapeDtypeStruct((B, S, LANES), BF16),
        jax.ShapeDtypeStruct((B, IDX_HEADS, S), F32),
    )
    return pl.pallas_call(
        _attn_proj_kernel,
        grid=grid,
        in_specs=[
            pl.BlockSpec((1, T, D), lambda b, t: (b, t, 0)),
            pl.BlockSpec((1, 1, T), lambda b, t: (b, 0, t)),
            full(inv), full(wq), full(wk), full(wv), full(wiq), full(wsm),
        ],
        out_specs=(
            fm, fm, fm,
            pl.BlockSpec((1, T, ATT_WIDTH), lambda b, t: (b, t, 0)),
            pl.BlockSpec((1, T, LANES), lambda b, t: (b, t, 0)),
            pl.BlockSpec((1, IDX_HEADS, T), lambda b, t: (b, 0, t)),
        ),
        out_shape=out_shape,
        compiler_params=_cparams(("parallel", "parallel")),
        name="attn_proj",
    )(x, pos3, inv, wq, wk, wv, wiq, wsm)


GLA_PROJ_CHUNK = 1024


def _gla_proj_kernel(x_ref, w_ref, o_ref):
    xb = x_ref[0].astype(BF16)
    width = w_ref.shape[1]
    for c0 in range(0, width, GLA_PROJ_CHUNK):
        c1 = min(c0 + GLA_PROJ_CHUNK, width)
        o_ref[0, :, c0:c1] = _dot(xb, w_ref[:, c0:c1]).astype(BF16)


def _gla_proj(x, wg):
    B, S, D = x.shape
    T = min(PROJ_TOKENS, S)
    W = wg.shape[1]
    return pl.pallas_call(
        _gla_proj_kernel,
        grid=(B, S // T),
        in_specs=[pl.BlockSpec((1, T, D), lambda b, t: (b, t, 0)),
                  pl.BlockSpec(wg.shape, lambda b, t: (0, 0))],
        out_specs=pl.BlockSpec((1, T, W), lambda b, t: (b, t, 0)),
        out_shape=jax.ShapeDtypeStruct((B, S, W), BF16),
        compiler_params=_cparams(("parallel", "parallel")),
        name="gla_proj",
    )(x, wg)


def _sort_key(x):
    u = lax.bitcast_convert_type(x, jnp.int32)
    return u ^ ((u >> 31) & jnp.int32(0x7FFFFFFF))


def _sort_unkey(k):
    return lax.bitcast_convert_type(k ^ ((k >> 31) & jnp.int32(0x7FFFFFFF)), F32)


def _fold8(x):
    return x.reshape(x.shape[0] // SUBLANES, SUBLANES, x.shape[1])


REDUCE_WAYS = 4


def _tree_reduce(x3, op):
    g = x3.shape[0]
    if g > REDUCE_WAYS:
        reducer = {jnp.add: jnp.sum, jnp.maximum: jnp.max, jnp.minimum: jnp.min}[op]
        x3 = reducer(x3.reshape((g // REDUCE_WAYS, REDUCE_WAYS) + x3.shape[1:]), axis=0)
    while x3.shape[0] > 1:
        half = x3.shape[0] // 2
        x3 = op(x3[:half], x3[half:])
    return x3[0]


def _dsa_kernel(topk, iqT_ref, aqT_ref, wT_ref, kidx_ref, k_ref, vT_ref, o_ref,
                sc_ref, bias_ref, lg_ref, iqz_ref, aqz_ref, mrun_ref, macc_ref, l_ref, acc_ref):
    QB = DSA_QB
    TS = DSA_TS
    S = sc_ref.shape[0]
    qi = pl.program_id(1)
    n_tiles = qi + 1
    q0 = qi * QB

    zeros64 = jnp.zeros((IDX_DIM, QB), BF16)
    for h in range(IDX_HEADS):
        iqz_ref[h, 0:IDX_DIM, :] = iqT_ref[0, h * IDX_DIM:(h + 1) * IDX_DIM, :]
        iqz_ref[h, IDX_DIM:, :] = zeros64
    for h in range(ATT_HEADS):
        lo_half = (h % 2) * ATT_HEAD_DIM
        hi_half = (1 - h % 2) * ATT_HEAD_DIM
        aqz_ref[h, lo_half:lo_half + ATT_HEAD_DIM, :] = aqT_ref[0, h * ATT_HEAD_DIM:(h + 1) * ATT_HEAD_DIM, :]
        aqz_ref[h, hi_half:hi_half + ATT_HEAD_DIM, :] = zeros64

    q_chunk = (q0 + lax.broadcasted_iota(jnp.int32, (1, QB), 1)) // CHUNK

    def score_tile(j, diag):
        s0 = pl.multiple_of(j * TS, TS)
        kt = kidx_ref[0, pl.ds(s0, TS), :]
        acc = jnp.zeros((TS, QB), F32)
        for h in range(IDX_HEADS):
            d = _dot(kt, iqz_ref[h])
            acc = acc + wT_ref[0, h:h + 1, :] * jnp.maximum(d, 0.0)
        if diag:
            s_chunk = (s0 + lax.broadcasted_iota(jnp.int32, (TS, 1), 0)) // CHUNK
            adm = s_chunk <= q_chunk
            lo_t = jnp.min(_fold8(jnp.where(adm, acc, jnp.inf)), axis=0)
            hi_t = jnp.max(_fold8(jnp.where(adm, acc, -jnp.inf)), axis=0)
            acc = jnp.where(adm, acc, -jnp.inf)
        else:
            lo_t = jnp.min(_fold8(acc), axis=0)
            hi_t = jnp.max(_fold8(acc), axis=0)
        sc_ref[pl.ds(s0, TS), :] = acc
        return lo_t, hi_t

    def p1_body(j, carry):
        lo8, hi8 = carry
        lo_t, hi_t = score_tile(j, False)
        return jnp.minimum(lo8, lo_t), jnp.maximum(hi8, hi_t)

    lo8 = jnp.full((SUBLANES, QB), jnp.inf, F32)
    hi8 = jnp.full((SUBLANES, QB), -jnp.inf, F32)
    lo8, hi8 = lax.fori_loop(0, qi, p1_body, (lo8, hi8))
    lo_t, hi_t = score_tile(qi, True)
    smin = jnp.min(jnp.minimum(lo8, lo_t), axis=0, keepdims=True)
    smax = jnp.max(jnp.maximum(hi8, hi_t), axis=0, keepdims=True)

    kf = float(topk)
    n_adm = ((q_chunk + 1) * CHUNK).astype(F32)
    need = n_adm > kf

    @pl.when(qi + 1 < S // TS)
    def _():
        sc_ref[pl.ds(pl.multiple_of((qi + 1) * TS, TS), TS), :] = jnp.full((TS, QB), -jnp.inf, F32)

    CT = 2 * TS

    def count_tiles(pred_fn):
        def body(j, c8):
            s0 = pl.multiple_of(j * CT, CT)
            ind = jnp.where(pred_fn(sc_ref[pl.ds(s0, CT), :], s0), 1.0, 0.0)
            return c8 + _tree_reduce(_fold8(ind), jnp.add)
        c8 = lax.fori_loop(0, (n_tiles + 1) // 2, body, jnp.zeros((SUBLANES, QB), F32))
        return jnp.sum(c8, axis=0, keepdims=True)

    def search_cond(st):
        return st[1] > 0

    def search_body(st):
        it, _, lo_i, hi_i, c_lo, c_hi, thr, open_ = st
        collapsed = (hi_i - 1) <= lo_i
        lo_f = _sort_unkey(lo_i)
        hi_f = _sort_unkey(hi_i)
        la = jnp.log(c_lo)
        frac = (la - np.log(kf)) / (la - jnp.log(jnp.maximum(c_hi, 0.5)))
        frac = jnp.clip(frac, 0.02, 0.98)
        guess_i = _sort_key(lo_f + (hi_f - lo_f) * frac)
        half_i = (lo_i >> 1) + (hi_i >> 1) + (lo_i & hi_i & 1)
        mid_i = jnp.where(jnp.logical_or(it % 3 == 2, it >= SEARCH_INTERP_ITERS), half_i, guess_i)
        mid_i = jnp.minimum(jnp.maximum(mid_i, lo_i + 1), hi_i - 1)
        t = _sort_unkey(mid_i)
        c = count_tiles(lambda tile, s0: tile >= t)
        live = jnp.logical_and(open_ > 0.0, jnp.logical_not(collapsed))
        found = jnp.logical_and(live, c == kf)
        up = jnp.logical_and(live, c > kf)
        dn = jnp.logical_and(live, c < kf)
        thr = jnp.where(found, t, thr)
        lo_i = jnp.where(up, mid_i, lo_i)
        c_lo = jnp.where(up, c, c_lo)
        hi_i = jnp.where(dn, mid_i, hi_i)
        c_hi = jnp.where(dn, c, c_hi)
        open_ = jnp.where(jnp.logical_or(up, dn), 1.0, 0.0)
        return it + 1, jnp.sum(open_).astype(jnp.int32), lo_i, hi_i, c_lo, c_hi, thr, open_

    zero_f = jnp.zeros((1, QB), F32)
    tiny_f = jnp.full((1, QB), jnp.finfo(F32).tiny, F32)
    c_zero = count_tiles(lambda tile, s0: tile >= zero_f)
    c_tiny = count_tiles(lambda tile, s0: tile >= tiny_f)
    key_zero = _sort_key(zero_f)
    key_tiny = _sort_key(tiny_f)
    pos_side = c_tiny > kf
    neg_side = c_zero < kf
    zero_tie = jnp.logical_and(c_zero > kf, c_tiny < kf)
    lo_i0 = jnp.where(pos_side, key_tiny, jnp.where(zero_tie, key_zero, _sort_key(smin)))
    c_lo0 = jnp.where(pos_side, c_tiny, jnp.where(zero_tie, c_zero, n_adm))
    hi_i0 = jnp.where(neg_side, key_zero, jnp.where(zero_tie, key_zero + 1, _sort_key(smax) + 1))
    c_hi0 = jnp.where(neg_side, c_zero, jnp.where(zero_tie, c_tiny, 0.0))
    thr0 = jnp.full((1, QB), jnp.finfo(F32).min, F32)
    thr0 = jnp.where(jnp.logical_and(need, c_zero == kf), zero_f, thr0)
    thr0 = jnp.where(jnp.logical_and(need, c_tiny == kf), tiny_f, thr0)
    open0 = jnp.where(jnp.logical_and(need, jnp.logical_or(pos_side, neg_side)), 1.0, 0.0)
    st = (jnp.int32(0), jnp.sum(open0).astype(jnp.int32), lo_i0, hi_i0, c_lo0, c_hi0, thr0, open0)
    _, _, lo_i, hi_i, c_lo, c_hi, thr, _ = lax.while_loop(
        search_cond, lambda s_: search_body(search_body(s_)), st)

    unresolved = thr == jnp.finfo(F32).min
    tied = jnp.logical_and(jnp.logical_and(need, unresolved), (hi_i - 1) <= lo_i)
    v = _sort_unkey(lo_i)
    thr = jnp.where(tied, v, thr)

    @pl.when(jnp.sum(jnp.where(tied, 1.0, 0.0)) > 0.0)
    def _():
        want = kf - c_hi

        def idx_cond(st):
            return st[1] > 0

        def idx_body(st):
            it, _, j_lo, j_hi, c_jl, c_jh, j_keep, open_ = st
            frac = (want - c_jl) / jnp.maximum(c_jh - c_jl, 1.0)
            guess = j_lo + ((j_hi - j_lo).astype(F32) * frac).astype(jnp.int32)
            mid = jnp.where(it % 3 == 2, (j_lo + j_hi) >> 1, guess)
            mid = jnp.minimum(jnp.maximum(mid, j_lo + 1), j_hi - 1)
            c = count_tiles(lambda tile, s0: jnp.logical_and(
                tile == v, (s0 + lax.broadcasted_iota(jnp.int32, (CT, 1), 0)) <= mid))
            live = open_ > 0.0
            below = jnp.logical_and(live, c < want)
            above = jnp.logical_and(live, c > want)
            j_keep = jnp.where(jnp.logical_and(live, c >= want), mid, j_keep)
            j_lo = jnp.where(below, mid, j_lo)
            c_jl = jnp.where(below, c, c_jl)
            j_hi = jnp.where(above, mid, j_hi)
            c_jh = jnp.where(above, c, c_jh)
            still = jnp.logical_and(jnp.logical_or(below, above), (j_hi - j_lo) > 1)
            open_ = jnp.where(still, 1.0, 0.0)
            return it + 1, jnp.sum(open_).astype(jnp.int32), j_lo, j_hi, c_jl, c_jh, j_keep, open_

        open_t = jnp.where(tied, 1.0, 0.0)
        st_t = (jnp.int32(0), jnp.sum(open_t).astype(jnp.int32),
                jnp.full((1, QB), -1, jnp.int32), jnp.full((1, QB), S - 1, jnp.int32),
                jnp.zeros((1, QB), F32), c_lo - c_hi, jnp.full((1, QB), S - 1, jnp.int32), open_t)
        j_keep = lax.while_loop(idx_cond, idx_body, st_t)[6]
        j_keep = jnp.where(tied, j_keep, S)

        def knock_body(j, carry):
            s0 = pl.multiple_of(j * TS, TS)
            tile = sc_ref[pl.ds(s0, TS), :]
            idx = s0 + lax.broadcasted_iota(jnp.int32, (TS, 1), 0)
            drop = jnp.logical_and(jnp.logical_and(tile == v, idx > j_keep), tied)
            sc_ref[pl.ds(s0, TS), :] = jnp.where(drop, -jnp.inf, tile)
            return carry

        lax.fori_loop(0, n_tiles, knock_body, 0)

    mrun_ref[...] = jnp.full(mrun_ref.shape, NEG_BIG, F32)
    macc_ref[...] = jnp.full(macc_ref.shape, NEG_BIG, F32)
    l_ref[...] = jnp.zeros(l_ref.shape, F32)
    acc_ref[...] = jnp.zeros(acc_ref.shape, F32)

    def set_bias(jt, slot):
        s0 = pl.multiple_of(jnp.minimum(jt, qi) * TS, TS)
        thr_eff = jnp.where(jt <= qi, thr, jnp.inf)
        bias_ref[slot] = jnp.where(sc_ref[pl.ds(s0, TS), :] >= thr_eff, 0.0, NEG_BIG)

    def logits_head(jt, slot, h):
        s0 = pl.multiple_of(jnp.minimum(jt, qi) * TS, TS)
        hp = h // 2
        kt = k_ref[0, pl.ds(s0, TS), hp * LANES:(hp + 1) * LANES]
        lg = _dot(kt, aqz_ref[h]) + bias_ref[slot]
        lg_ref[slot, h] = lg
        tile_max = jnp.max(_tree_reduce(_fold8(lg), jnp.maximum), axis=0, keepdims=True)
        return jnp.maximum(mrun_ref[h:h + 1, :], tile_max)

    def value_head(jt, slot, h):
        s0 = pl.multiple_of(jnp.minimum(jt, qi) * TS, TS)
        m_cur = mrun_ref[h:h + 1, :]
        alpha = jnp.exp2(macc_ref[h:h + 1, :] - m_cur)
        macc_ref[h:h + 1, :] = m_cur
        p = jnp.exp2(lg_ref[slot, h] - m_cur)
        l_ref[h:h + 1, :] = alpha * l_ref[h:h + 1, :] + jnp.sum(_tree_reduce(_fold8(p), jnp.add), axis=0,
                                                                 keepdims=True)
        vt = vT_ref[0, h * ATT_HEAD_DIM:(h + 1) * ATT_HEAD_DIM, pl.ds(s0, TS)]
        rows = slice(h * ATT_HEAD_DIM, (h + 1) * ATT_HEAD_DIM)
        acc_ref[rows, :] = alpha * acc_ref[rows, :] + _dot(vt, p.astype(BF16))

    def half_step(j_value, slot_value):
        set_bias(j_value + 1, 1 - slot_value)
        for h in range(ATT_HEADS):
            m_next = logits_head(j_value + 1, 1 - slot_value, h)
            value_head(j_value, slot_value, h)
            mrun_ref[h:h + 1, :] = m_next

    set_bias(0, 0)
    for h in range(ATT_HEADS):
        mrun_ref[h:h + 1, :] = logits_head(0, 0, h)

    def p3_body(i, carry):
        half_step(2 * i, 0)
        half_step(2 * i + 1, 1)
        return carry

    lax.fori_loop(0, (n_tiles + 1) // 2, p3_body, 0)

    for h in range(ATT_HEADS):
        rows = slice(h * ATT_HEAD_DIM, (h + 1) * ATT_HEAD_DIM)
        acc_ref[rows, :] = acc_ref[rows, :] * (1.0 / l_ref[h:h + 1, :])
    o_ref[0] = acc_ref[...].T.astype(BF16)


def _dsa(iqT, aqT, wT, kidx, k, vT):
    B, W, S = aqT.shape
    QB, TS = DSA_QB, DSA_TS
    assert S % QB == 0 and QB == TS and QB % CHUNK == 0
    topk = min(TOPK_MAX, S // 4)
    qblk = pl.BlockSpec((1, W, QB), lambda b, q: (b, 0, q))
    return pl.pallas_call(
        functools.partial(_dsa_kernel, topk),
        grid=(B, S // QB),
        in_specs=[
            qblk, qblk,
            pl.BlockSpec((1, IDX_HEADS, QB), lambda b, q: (b, 0, q)),
            pl.BlockSpec((1, S, LANES), lambda b, q: (b, 0, 0)),
            pl.BlockSpec((1, S, W), lambda b, q: (b, 0, 0)),
            pl.BlockSpec((1, W, S), lambda b, q: (b, 0, 0)),
        ],
        out_specs=pl.BlockSpec((1, QB, W), lambda b, q: (b, q, 0)),
        out_shape=jax.ShapeDtypeStruct((B, S, W), BF16),
        scratch_shapes=[
            pltpu.VMEM((S, QB), F32),
            pltpu.VMEM((2, TS, QB), F32),
            pltpu.VMEM((2, ATT_HEADS, TS, QB), F32),
            pltpu.VMEM((IDX_HEADS, LANES, QB), BF16),
            pltpu.VMEM((ATT_HEADS, LANES, QB), BF16),
            pltpu.VMEM((ATT_HEADS, QB), F32),
            pltpu.VMEM((ATT_HEADS, QB), F32),
            pltpu.VMEM((ATT_HEADS, QB), F32),
            pltpu.VMEM((W, QB), F32),
        ],
        compiler_params=_cparams(("parallel", "arbitrary")),
        name="dsa",
    )(iqT, aqT, wT, kidx, k, vT)


def _gla_kernel(gq_ref, gk_ref, gv_ref, glr_ref, gr_ref, wgate_ref, bgate_ref, gnorm_ref, tri_ref,
                o_ref, state_ref, obuf_ref):
    T = gq_ref.shape[1]
    t = pl.program_id(1)

    @pl.when(t == 0)
    def _():
        state_ref[...] = jnp.zeros(state_ref.shape, F32)

    gl = _dot(glr_ref[0], wgate_ref[...]) + bgate_ref[...]
    log_a = (jnp.minimum(gl, 0.0) - jnp.log1p(jnp.exp(-jnp.abs(gl)))) * (1.0 / GLA_TAU)
    tri = tri_ref[...]

    for c in range(T // CHUNK):
        r0 = c * CHUNK
        la = log_a[r0:r0 + CHUNK]
        cum = jnp.dot(tri, la, preferred_element_type=F32, precision=lax.Precision.HIGHEST)
        total = cum[CHUNK - 1:CHUNK, :]
        kdec = gk_ref[0, r0:r0 + CHUNK, :].astype(F32) * jnp.exp(total - cum)
        a_tot = jnp.exp(total)
        qc = gq_ref[0, r0:r0 + CHUNK, :]
        for h in range(GLA_HEADS):
            kcols = slice(h * GLA_DK, (h + 1) * GLA_DK)
            vcols = slice(h * GLA_DV, (h + 1) * GLA_DV)
            vh = gv_ref[0, r0:r0 + CHUNK, vcols]
            upd = _dot_tn(vh, kdec[:, kcols].astype(BF16))
            st = a_tot[:, kcols] * state_ref[h] + upd
            state_ref[h] = st
            oh = _dot_nt(qc[:, kcols], st.astype(BF16)) * (GLA_DK ** -0.5)
            obuf_ref[r0:r0 + CHUNK, vcols] = oh

    for h in range(GLA_HEADS):
        vcols = slice(h * GLA_DV, (h + 1) * GLA_DV)
        o = obuf_ref[:, vcols]
        mu = jnp.mean(o, axis=-1, keepdims=True)
        oc = o - mu
        var = jnp.mean(oc * oc, axis=-1, keepdims=True)
        on = oc * lax.rsqrt(var + LN_EPS) * gnorm_ref[:, vcols]
        g = gr_ref[0, :, vcols].astype(F32)
        o_ref[0, :, vcols] = (on * (g * jax.nn.sigmoid(g))).astype(BF16)


def _gla(projg, wgate, bgate, gnorm, tri):
    B, S, _ = projg.shape
    T = min(GLA_TOKENS, S)
    kb = GLA_K_WIDTH
    vb = GLA_V_WIDTH
    full = lambda a: pl.BlockSpec(a.shape, lambda b, t: (0,) * a.ndim)
    return pl.pallas_call(
        _gla_kernel,
        grid=(B, S // T),
        in_specs=[
            pl.BlockSpec((1, T, kb), lambda b, t: (b, t, 0)),
            pl.BlockSpec((1, T, kb), lambda b, t: (b, t, 1)),
            pl.BlockSpec((1, T, vb), lambda b, t: (b, t, 1)),
            pl.BlockSpec((1, T, LANES), lambda b, t: (b, t, 5 * vb // LANES)),
            pl.BlockSpec((1, T, vb), lambda b, t: (b, t, 2)),
            full(wgate), full(bgate), full(gnorm), full(tri),
        ],
        out_specs=pl.BlockSpec((1, T, vb), lambda b, t: (b, t, 0)),
        out_shape=jax.ShapeDtypeStruct((B, S, vb), BF16),
        scratch_shapes=[pltpu.VMEM((GLA_HEADS, GLA_DV, GLA_DK), F32),
                        pltpu.VMEM((T, vb), F32)],
        compiler_params=_cparams(("parallel", "arbitrary")),
        name="gla",
    )(projg, projg, projg, projg, projg, wgate, bgate, gnorm, tri)


def _mix_kernel(x_ref, att_ref, og_ref, ga_ref, gb_ref, pa_ref, pg_ref, wo_ref, bo_ref, g_ref, b_ref, o_ref):
    ya = _dot(att_ref[...], pa_ref[...])
    yb = _dot(og_ref[...], pg_ref[...])
    merged = jax.nn.sigmoid(ga_ref[...].astype(F32)) * ya + jax.nn.sigmoid(gb_ref[...].astype(F32)) * yb
    y = _dot(merged.astype(BF16), wo_ref[...]) + bo_ref[...]
    o_ref[...] = _layer_norm_rows(DN_ALPHA * x_ref[...] + y, g_ref[...], b_ref[...])


def _mix(x2, att2, og2, projg2, pa, pg, wo, bo, g, b):
    N, D = x2.shape
    T = min(MIX_TOKENS, N)
    full = lambda a: pl.BlockSpec(a.shape, lambda i: (0,) * a.ndim)
    return pl.pallas_call(
        _mix_kernel,
        grid=(N // T,),
        in_specs=[
            pl.BlockSpec((T, D), lambda i: (i, 0)),
            pl.BlockSpec((T, ATT_WIDTH), lambda i: (i, 0)),
            pl.BlockSpec((T, GLA_V_WIDTH), lambda i: (i, 0)),
            pl.BlockSpec((T, D), lambda i: (i, 3)),
            pl.BlockSpec((T, D), lambda i: (i, 4)),
            full(pa), full(pg), full(wo), full(bo), full(g), full(b),
        ],
        out_specs=pl.BlockSpec((T, D), lambda i: (i, 0)),
        out_shape=jax.ShapeDtypeStruct((N, D), F32),
        compiler_params=_cparams(("parallel",)),
        name="mix_out",
    )(x2, att2, og2, projg2, projg2, pa, pg, wo, bo, g, b)


def _causal_conv3(u, halo_u, cw, cb):
    T = u.shape[0]
    row = lax.broadcasted_iota(jnp.int32, (T, 1), 0)
    u1 = jnp.where(row == 0, halo_u[7:8, :], pltpu.roll(u, 1, 0))
    r2 = pltpu.roll(u, 2, 0)
    u2 = jnp.where(row == 0, halo_u[6:7, :], jnp.where(row == 1, halo_u[7:8, :], r2))
    return cw[0:1, :] * u2 + cw[1:2, :] * u1 + cw[2:3, :] * u + cb


def _ffn_kernel(x_ref, halo_ref, wa_ref, wb_ref, cwa_ref, cwb_ref, cba_ref, cbb_ref, wd_ref, g_ref, b_ref,
                o_ref, acc_ref):
    t = pl.program_id(1)
    j = pl.program_id(2)
    nj = pl.num_programs(2)
    x = x_ref[0]
    halo = jnp.where(t == 0, 0.0, halo_ref[0])
    xe = jnp.concatenate([halo, x], axis=0).astype(BF16)
    ua = _dot(xe, wa_ref[...])
    ub = _dot(xe, wb_ref[...])
    a = _causal_conv3(ua[SUBLANES:], ua[:SUBLANES], cwa_ref[...], cba_ref[...])
    bb = _causal_conv3(ub[SUBLANES:], ub[:SUBLANES], cwb_ref[...], cbb_ref[...])
    hmid = (jax.nn.gelu(a) * bb).astype(BF16)
    part = _dot(hmid, wd_ref[...])

    @pl.when(j == 0)
    def _():
        acc_ref[...] = part

    @pl.when(j > 0)
    def _():
        acc_ref[...] = acc_ref[...] + part

    @pl.when(j == nj - 1)
    def _():
        o_ref[0] = _layer_norm_rows(DN_ALPHA * x + acc_ref[...], g_ref[...], b_ref[...])


def _ffn(x, wa, wb, cwa, cwb, cba, cbb, wd, g, b, ff_tile):
    B, S, D = x.shape
    T = min(FFN_TOKENS, S)
    F = wa.shape[1]
    nj = F // ff_tile
    hblk = T // SUBLANES
    colblk = lambda rows: pl.BlockSpec((rows, ff_tile), lambda bb_, t, j: (0, j))
    full = lambda a: pl.BlockSpec(a.shape, lambda bb_, t, j: (0,) * a.ndim)
    return pl.pallas_call(
        _ffn_kernel,
        grid=(B, S // T, nj),
        in_specs=[
            pl.BlockSpec((1, T, D), lambda bb_, t, j: (bb_, t, 0)),
            pl.BlockSpec((1, SUBLANES, D), lambda bb_, t, j: (bb_, jnp.maximum(t * hblk - 1, 0), 0)),
            colblk(D), colblk(D),
            colblk(CONV_WIDTH), colblk(CONV_WIDTH), colblk(1), colblk(1),
            pl.BlockSpec((ff_tile, D), lambda bb_, t, j: (j, 0)),
            full(g), full(b),
        ],
        out_specs=pl.BlockSpec((1, T, D), lambda bb_, t, j: (bb_, t, 0)),
        out_shape=jax.ShapeDtypeStruct((B, S, D), F32),
        scratch_shapes=[pltpu.VMEM((T, D), F32)],
        compiler_params=_cparams(("parallel", "parallel", "arbitrary")),
        name="conv_ffn",
    )(x, x, wa, wb, cwa, cwb, cba, cbb, wd, g, b)


def _ff_tile(d_ff):
    for cand in (1408, 1024, 512, 256, 128):
        if d_ff % cand == 0:
            return cand
    return d_ff


def kernel(x, positions, w_in, gla_w_gate, gla_b_gate, gla_norm_g, p_attn, p_gla, w_mix_out, b_mix_out,
           ln1_g, ln1_b, w_up, conv_w, conv_b, w_down, ln2_g, ln2_b):
    B, S, D = x.shape
    depth = w_in.shape[0]
    d_ff = w_down.shape[1]
    pos3 = positions.reshape(B, 1, S)
    inv = (ROPE_THETA ** (-jnp.arange(ROPE_HALF, dtype=F32) * 2.0 / ROPE_DIMS)).reshape(ROPE_HALF, 1)
    tri = jnp.tril(jnp.ones((CHUNK, CHUNK), F32))
    ff_tile = _ff_tile(d_ff)

    o_aq, o_ak, o_av, o_iq = 0, ATT_WIDTH, 2 * ATT_WIDTH, 3 * ATT_WIDTH
    o_ik = 4 * ATT_WIDTH
    o_iw = o_ik + IDX_DIM
    o_gq = o_iw + IDX_HEADS
    o_gk = o_gq + GLA_K_WIDTH
    o_gv = o_gk + GLA_K_WIDTH
    o_glr = o_gv + GLA_V_WIDTH
    o_gr = o_glr + GLA_GATE_RANK
    o_ga = o_gr + GLA_V_WIDTH
    o_gb = o_ga + D
    o_end = o_gb + D

    for l in range(depth):
        w = w_in[l]
        tw = lambda c0, c1: w[:, c0:c1].T.astype(BF16)
        wsm = jnp.concatenate([w[:, o_ik:o_gq].T,
                               jnp.zeros((LANES - IDX_DIM - IDX_HEADS, D), F32)], axis=0).astype(BF16)
        wg = jnp.concatenate([w[:, o_gq:o_glr], w[:, o_gr:o_end], w[:, o_glr:o_gr],
                              jnp.zeros((D, LANES - GLA_GATE_RANK), F32)], axis=1).astype(BF16)
        wgate = jnp.concatenate([gla_w_gate[l], jnp.zeros((LANES - GLA_GATE_RANK, GLA_K_WIDTH), F32)],
                                axis=0).astype(BF16)

        aqT, iqT, vT, k, kidx, wT = _attn_proj(x, pos3, inv, tw(o_aq, o_ak), tw(o_ak, o_av), tw(o_av, o_iq),
                                               tw(o_iq, o_ik), wsm)
        projg = _gla_proj(x, wg)
        att = _dsa(iqT, aqT, wT, kidx, k, vT)
        og = _gla(projg, wgate, gla_b_gate[l].reshape(1, -1), gla_norm_g[l].reshape(1, -1), tri)
        x1 = _mix(x.reshape(B * S, D), att.reshape(B * S, ATT_WIDTH), og.reshape(B * S, GLA_V_WIDTH),
                  projg.reshape(B * S, -1), p_attn[l].astype(BF16), p_gla[l].astype(BF16),
                  w_mix_out[l].astype(BF16), b_mix_out[l].reshape(1, D), ln1_g[l].reshape(1, D),
                  ln1_b[l].reshape(1, D)).reshape(B, S, D)
        wu = w_up[l].astype(BF16)
        x = _ffn(x1, wu[:, :d_ff], wu[:, d_ff:], conv_w[l][:, :d_ff], conv_w[l][:, d_ff:],
                 conv_b[l][:d_ff].reshape(1, d_ff), conv_b[l][d_ff:].reshape(1, d_ff),
                 w_down[l].astype(BF16), ln2_g[l].reshape(1, D), ln2_b[l].reshape(1, D), ff_tile)
    return x
```
